```python
import math
import jax, jax.numpy as jnp
from jax import lax
import numpy as np

D_MODEL = 1024
BATCH = 16
SEQ = 2048
DEPTH = 1

CHUNK = 64
Q_BLOCK = 128
HEAD_DIM_A = 64
N_HEADS_A = D_MODEL // (2 * HEAD_DIM_A)
ATTN_WIDTH = N_HEADS_A * 2 * HEAD_DIM_A
ROPE_THETA = 10000.0
POOL_WINDOWS = (2, 4, 8, 16)
POOL_GROUPS = len(POOL_WINDOWS)
POOL_WIDTH = D_MODEL // 2
POOL_GROUP_DIM = POOL_WIDTH // POOL_GROUPS
N_BRANCHES = 2
IN_WIDTH = 3 * ATTN_WIDTH + POOL_WIDTH + N_BRANCHES * D_MODEL
SPLIT_POINTS = (ATTN_WIDTH, 2 * ATTN_WIDTH, 3 * ATTN_WIDTH, 3 * ATTN_WIDTH + POOL_WIDTH)
N_GROUPS = 4
EXPERTS_PER_GROUP = 8
N_EXPERTS = N_GROUPS * EXPERTS_PER_GROUP
TOP_K = 2
EXPERT_FF = D_MODEL // 2
MOE_BLOCK = 256
LN_EPS = 1e-5
RMS_EPS = 1e-5
ALPHA = (2.0 * DEPTH) ** 0.25
BETA = (8.0 * DEPTH) ** -0.25
NEG_INF = -1e30

kernel_name = 'hybrid_diffattn_pool_hmoe_deepnorm'


def _layer_norm(x, gain, bias):
    xf = x.astype(jnp.float32)
    mu = jnp.mean(xf, axis=-1, keepdims=True)
    var = jnp.mean(jnp.square(xf - mu), axis=-1, keepdims=True)
    y = (xf - mu) * lax.rsqrt(var + LN_EPS)
    return (y * gain.astype(jnp.float32) + bias.astype(jnp.float32)).astype(x.dtype)


def _rope(t, positions):
    half = HEAD_DIM_A // 2
    inv_freq = ROPE_THETA ** (-jnp.arange(half, dtype=jnp.float32) * (2.0 / HEAD_DIM_A))
    ang = positions.astype(jnp.float32)[..., None] * inv_freq
    cos = jnp.cos(ang)[:, :, None, None, :].astype(t.dtype)
    sin = jnp.sin(ang)[:, :, None, None, :].astype(t.dtype)
    t1, t2 = t[..., :half], t[..., half:]
    return jnp.concatenate([t1 * cos - t2 * sin, t2 * cos + t1 * sin], axis=-1)


def _diff_attention(q, k, v, lam):
    B, S = q.shape[0], q.shape[1]
    nb = S // Q_BLOCK
    scale = HEAD_DIM_A ** -0.5
    q_blocks = q.reshape(B, nb, Q_BLOCK, N_HEADS_A, 2, HEAD_DIM_A).transpose(1, 0, 3, 4, 2, 5)
    k_t = k.transpose(0, 2, 3, 1, 4)
    v_t = v.transpose(0, 2, 1, 3)
    key_chunk = jnp.arange(S) // CHUNK

    def one_block(args):
        q_blk, blk = args
        s = jnp.einsum('bhmqd,bhmkd->bhmqk', q_blk, k_t).astype(jnp.float32) * scale
        q_chunk = (blk * Q_BLOCK + jnp.arange(Q_BLOCK)) // CHUNK
        allowed = key_chunk[None, :] <= q_chunk[:, None]
        p = jax.nn.softmax(jnp.where(allowed, s, NEG_INF), axis=-1)
        a = p[:, :, 0] - lam * p[:, :, 1]
        return jnp.einsum('bhqk,bhkv->bhqv', a.astype(v_t.dtype), v_t)

    o = lax.map(one_block, (q_blocks, jnp.arange(nb)))
    return o.transpose(1, 0, 3, 2, 4).reshape(B, S, N_HEADS_A, 2 * HEAD_DIM_A)


def _multiscale_pool(u, w_pool, pool_scale):
    B, S = u.shape[0], u.shape[1]
    groups = u.astype(jnp.float32).reshape(B, S, POOL_GROUPS, POOL_GROUP_DIM)
    cs = jnp.cumsum(groups, axis=1)
    t = jnp.arange(S)
    means = []
    for g, w in enumerate(POOL_WINDOWS):
        c = cs[:, :, g]
        lagged = jnp.pad(c, ((0, 0), (w, 0), (0, 0)))[:, :S]
        count = jnp.minimum(t + 1, w).astype(jnp.float32)[None, :, None]
        means.append((c - lagged) / count)
    pooled = (jnp.stack(means, axis=2) - groups).astype(u.dtype)
    mixed = jnp.einsum('bsgc,gcd->bsgd', pooled, w_pool)
    return mixed.reshape(B, S, POOL_WIDTH) * pool_scale


def _token_mixer(h, positions, w_in, gate_bias, lam_vecs, subln_gain, lambda_init,
                 w_pool, pool_scale, w_branch_a, w_branch_b, w_out):
    B, S = h.shape[0], h.shape[1]
    proj = h @ w_in
    q, k, v, u, gates = jnp.split(proj, SPLIT_POINTS, axis=-1)
    q = _rope(q.reshape(B, S, N_HEADS_A, 2, HEAD_DIM_A), positions)
    k = _rope(k.reshape(B, S, N_HEADS_A, 2, HEAD_DIM_A), positions)
    v = v.reshape(B, S, N_HEADS_A, 2 * HEAD_DIM_A)
    lv = lam_vecs.astype(jnp.float32)
    lam = jnp.exp(jnp.sum(lv[0] * lv[1])) - jnp.exp(jnp.sum(lv[2] * lv[3])) + lambda_init
    o = _diff_attention(q, k, v, lam).astype(jnp.float32)
    o = o * lax.rsqrt(jnp.mean(jnp.square(o), axis=-1, keepdims=True) + RMS_EPS)
    o = o * subln_gain.astype(jnp.float32) * (1.0 - lambda_init)
    y_a = o.astype(h.dtype).reshape(B, S, ATTN_WIDTH) @ w_branch_a
    y_b = _multiscale_pool(u, w_pool, pool_scale) @ w_branch_b
    g = jax.nn.sigmoid((gates + gate_bias).astype(jnp.float32)).reshape(B, S, N_BRANCHES, D_MODEL)
    merged = g[:, :, 0] * y_a.astype(jnp.float32) + g[:, :, 1] * y_b.astype(jnp.float32)
    return merged.astype(h.dtype) @ w_out


def _hierarchical_moe(h, w_router_group, b_router_group, w_router_expert, b_router_expert,
                      w_gate_up, w_down):
    B, S, D = h.shape
    T = B * S
    x2 = h.reshape(T, D)
    g_prob = jax.nn.softmax((x2 @ w_router_group + b_router_group).astype(jnp.float32), axis=-1)
    g_p, g_idx = lax.top_k(g_prob, 1)
    e_logit = (x2 @ w_router_expert + b_router_expert).astype(jnp.float32)
    e_logit = e_logit.reshape(T, N_GROUPS, EXPERTS_PER_GROUP)
    e_logit = jnp.take_along_axis(e_logit, g_idx[:, :, None], axis=1)[:, 0]
    e_p, e_local = lax.top_k(jax.nn.softmax(e_logit, axis=-1), TOP_K)
    e_p = e_p / jnp.sum(e_p, axis=-1, keepdims=True)
    weights = g_p * e_p
    experts = g_idx * EXPERTS_PER_GROUP + e_local

    A = T * TOP_K
    n_blocks = -(-A // MOE_BLOCK) + N_EXPERTS
    P = n_blocks * MOE_BLOCK
    flat_e = experts.reshape(A)
    flat_w = weights.reshape(A)
    flat_tok = jnp.repeat(jnp.arange(T, dtype=jnp.int32), TOP_K)
    order = jnp.argsort(flat_e)
    sorted_e = flat_e[order]
    counts = jnp.bincount(flat_e, length=N_EXPERTS)
    padded = ((counts + MOE_BLOCK - 1) // MOE_BLOCK) * MOE_BLOCK
    ends = jnp.cumsum(padded)
    start_padded = ends - padded
    start = jnp.cumsum(counts) - counts
    dest = start_padded[sorted_e] + (jnp.arange(A) - start[sorted_e])
    row_tok = jnp.full((P,), T, dtype=jnp.int32).at[dest].set(flat_tok[order])
    row_w = jnp.zeros((P,), jnp.float32).at[dest].set(flat_w[order])
    block_expert = jnp.minimum(
        jnp.searchsorted(ends, jnp.arange(n_blocks) * MOE_BLOCK, side='right'), N_EXPERTS - 1)
    x_pad = jnp.concatenate([x2, jnp.zeros((1, D), x2.dtype)], axis=0)
    xb = x_pad[row_tok].reshape(n_blocks, MOE_BLOCK, D)

    def run_block(args):
        xs, e = args
        gate, up = jnp.split(xs @ w_gate_up[e], 2, axis=-1)
        return (jax.nn.silu(gate) * up) @ w_down[e]

    yb = lax.map(run_block, (xb, block_expert)).reshape(P, D)
    out = jax.ops.segment_sum(yb.astype(jnp.float32) * row_w[:, None], row_tok, num_segments=T + 1)
    return out[:T].astype(h.dtype).reshape(B, S, D)


def setup_inputs(seed: int = 0) -> dict:
    key = jax.random.key(seed)
    ks = jax.random.split(key, 24)
    nrm = lambda k, shape, s: jax.random.normal(k, shape, jnp.float32) * s
    L, D = DEPTH, D_MODEL
    x = jax.random.normal(ks[0], (BATCH, SEQ, D), jnp.float32)
    offset = jax.random.randint(ks[1], (BATCH, 1), 0, 4096, dtype=jnp.int32)
    positions = offset + jnp.arange(SEQ, dtype=jnp.int32)[None, :]
    w_q = nrm(ks[2], (L, D, ATTN_WIDTH), D ** -0.5)
    w_k = nrm(ks[3], (L, D, ATTN_WIDTH), D ** -0.5)
    w_v = nrm(ks[4], (L, D, ATTN_WIDTH), BETA * D ** -0.5)
    w_u = nrm(ks[5], (L, D, POOL_WIDTH), D ** -0.5)
    w_g = nrm(ks[6], (L, D, N_BRANCHES * D), D ** -0.5)
    w_in = jnp.concatenate([w_q, w_k, w_v, w_u, w_g], axis=-1)
    return {
        'x': x,
        'positions': positions,
        'w_in': w_in,
        'gate_bias': nrm(ks[7], (L, N_BRANCHES * D), 0.01),
        'lam_vecs': nrm(ks[8], (L, 4, HEAD_DIM_A), 0.1),
        'subln_gain': 1.0 + nrm(ks[9], (L, 2 * HEAD_DIM_A), 0.02),
        'w_pool': nrm(ks[10], (L, POOL_GROUPS, POOL_GROUP_DIM, POOL_GROUP_DIM), POOL_GROUP_DIM ** -0.5),
        'pool_scale': 1.0 + nrm(ks[11], (L, POOL_WIDTH), 0.02),
        'w_branch_a': nrm(ks[12], (L, ATTN_WIDTH, D), BETA * ATTN_WIDTH ** -0.5),
        'w_branch_b': nrm(ks[13], (L, POOL_WIDTH, D), BETA * POOL_WIDTH ** -0.5),
        'w_out': nrm(ks[14], (L, D, D), BETA * D ** -0.5),
        'ln1_gain': 1.0 + nrm(ks[15], (L, D), 0.02),
        'ln1_bias': nrm(ks[16], (L, D), 0.01),
        'w_router_group': nrm(ks[17], (L, D, N_GROUPS), D ** -0.5),
        'b_router_group': nrm(ks[18], (L, N_GROUPS), 0.01),
        'w_router_expert': nrm(ks[19], (L, D, N_EXPERTS), D ** -0.5),
        'b_router_expert': nrm(ks[20], (L, N_EXPERTS), 0.01),
        'w_gate_up': nrm(ks[21], (L, N_EXPERTS, D, 2 * EXPERT_FF), D ** -0.5),
        'w_down': nrm(ks[22], (L, N_EXPERTS, EXPERT_FF, D), BETA * EXPERT_FF ** -0.5),
        'ln2_gain': 1.0 + nrm(ks[23], (L, D), 0.02),
        'ln2_bias': nrm(jax.random.fold_in(ks[23], 1), (L, D), 0.01),
    }


def reference(x, positions, w_in, gate_bias, lam_vecs, subln_gain, w_pool, pool_scale,
              w_branch_a, w_branch_b, w_out, ln1_gain, ln1_bias, w_router_group,
              b_router_group, w_router_expert, b_router_expert, w_gate_up, w_down,
              ln2_gain, ln2_bias):
    for layer in range(DEPTH):
        lambda_init = 0.8 - 0.6 * math.exp(-0.3 * layer)
        mix = _token_mixer(x, positions, w_in[layer], gate_bias[layer], lam_vecs[layer],
                           subln_gain[layer], lambda_init, w_pool[layer], pool_scale[layer],
                           w_branch_a[layer], w_branch_b[layer], w_out[layer])
        x = _layer_norm(ALPHA * x + mix, ln1_gain[layer], ln1_bias[layer])
        ffn = _hierarchical_moe(x, w_router_group[layer], b_router_group[layer],
                                w_router_expert[layer], b_router_expert[layer],
                                w_gate_up[layer], w_down[layer])
        x = _layer_norm(ALPHA * x + ffn, ln2_gain[layer], ln2_bias[layer])
    return x
```

```python
import functools
import math

import jax
import jax.numpy as jnp
from jax import lax
from jax.experimental import pallas as pl
from jax.experimental.pallas import tpu as pltpu

HEAD_DIM = 64
VALUE_DIM = 2 * HEAD_DIM
CHUNK = 64
ROPE_THETA = 10000.0
POOL_WINDOWS = (2, 4, 8, 16)
POOL_HALO = 16
N_GROUPS = 4
EXPERTS_PER_GROUP = 8
N_EXPERTS = N_GROUPS * EXPERTS_PER_GROUP
TOP_K = 2
MOE_BLOCK = 256
LN_EPS = 1e-5
RMS_EPS = 1e-5
DEPTH = 1
ALPHA = (2.0 * DEPTH) ** 0.25
NEG_INF = -1e30

LANES = 128
SUBLANES = 8
VMEM_LIMIT_BYTES = 56 * 1024 * 1024

ATTN_TILE = 256
ROUTE_ROWS = 8
ROUTE_TILE = 256


def _params(semantics):
    return pltpu.CompilerParams(dimension_semantics=semantics, vmem_limit_bytes=VMEM_LIMIT_BYTES)


def _full(a):
    return pl.BlockSpec(a.shape, lambda *_: (0,) * a.ndim)


def _dot(a, b):
    return jnp.dot(a, b, preferred_element_type=jnp.float32)


def _dot_nt(a, b):
    return lax.dot_general(a, b, (((1,), (1,)), ((), ())), preferred_element_type=jnp.float32)


def _sigmoid(x):
    return 1.0 / (1.0 + jnp.exp(-x))


def _layer_norm(r, gain, bias):
    mu = jnp.mean(r, axis=-1, keepdims=True)
    c = r - mu
    var = jnp.mean(c * c, axis=-1, keepdims=True)
    return c * lax.rsqrt(var + LN_EPS) * gain + bias


def _qk_proj_kernel(x_ref, pos_ref, invf_ref, w_ref, o_ref, xb_ref, cos_ref, sin_ref, *, n_q_tiles):
    j = pl.program_id(1)

    @pl.when(j == 0)
    def _():
        xb_ref[...] = x_ref[...].astype(jnp.bfloat16)
        ang = pos_ref[...].astype(jnp.float32) * invf_ref[...]
        lane = lax.broadcasted_iota(jnp.int32, ang.shape, 1)
        first = (lane % HEAD_DIM) < (HEAD_DIM // 2)
        cos_ref[...] = jnp.cos(ang)
        sin_ref[...] = jnp.where(first, -jnp.sin(ang), jnp.sin(ang))

    acc = _dot(xb_ref[...], w_ref[...])
    scale = jnp.where(j < n_q_tiles, HEAD_DIM ** -0.5, 1.0)
    cos = cos_ref[...]
    sin = sin_ref[...]
    lane = lax.broadcasted_iota(jnp.int32, cos.shape, 1)
    first = (lane % HEAD_DIM) < (HEAD_DIM // 2)
    half = HEAD_DIM // 2
    for c in range(acc.shape[1] // LANES):
        t = acc[:, c * LANES:(c + 1) * LANES]
        partner = jnp.where(first, pltpu.roll(t, LANES - half, 1), pltpu.roll(t, half, 1))
        o_ref[:, c * LANES:(c + 1) * LANES] = ((t * cos + partner * sin) * scale).astype(o_ref.dtype)


def _qk_proj(x2, pos2, inv_freq, w_qk, tm, tn):
    T, D = x2.shape
    n_cols = w_qk.shape[1]
    kern = functools.partial(_qk_proj_kernel, n_q_tiles=(n_cols // 2) // tn)
    return pl.pallas_call(
        kern,
        grid=(T // tm, n_cols // tn),
        in_specs=[
            pl.BlockSpec((tm, D), lambda i, j: (i, 0)),
            pl.BlockSpec((tm, 1), lambda i, j: (i, 0)),
            pl.BlockSpec((1, LANES), lambda i, j: (0, 0)),
            pl.BlockSpec((D, tn), lambda i, j: (0, j)),
        ],
        out_specs=pl.BlockSpec((tm, tn), lambda i, j: (i, j)),
        out_shape=jax.ShapeDtypeStruct((T, n_cols), jnp.bfloat16),
        scratch_shapes=[
            pltpu.VMEM((tm, D), jnp.bfloat16),
            pltpu.VMEM((tm, LANES), jnp.float32),
            pltpu.VMEM((tm, LANES), jnp.float32),
        ],
        compiler_params=_params(("parallel", "arbitrary")),
        name="qk_proj",
    )(x2, pos2, inv_freq, w_qk)


def _v_proj_kernel(x_ref, w_ref, o_ref):
    xb = x_ref[...].astype(jnp.bfloat16)
    vt = _dot_nt(w_ref[...], xb)
    o_ref[...] = vt.reshape(o_ref.shape).astype(o_ref.dtype)


def _v_proj(x2, w_vT, batch, seq, kt):
    T, D = x2.shape
    n_heads = w_vT.shape[0] // VALUE_DIM
    tiles_per_seq = seq // kt
    return pl.pallas_call(
        _v_proj_kernel,
        grid=(T // kt,),
        in_specs=[pl.BlockSpec((kt, D), lambda i: (i, 0)), _full(w_vT)],
        out_specs=pl.BlockSpec((None, n_heads, None, VALUE_DIM, kt),
                               lambda i: (i // tiles_per_seq, 0, i % tiles_per_seq, 0, 0)),
        out_shape=jax.ShapeDtypeStruct((batch, n_heads, tiles_per_seq, VALUE_DIM, kt), jnp.bfloat16),
        compiler_params=_params(("parallel",)),
        name="v_proj",
    )(x2, w_vT)


def _pool_branch_kernel(x_ref, wu_ref, wg_ref, bg_ref, wpool_ref, pscale_ref, wb_ref, o_ref, ext_ref,
                        *, tiles_per_seq):
    i = pl.program_id(0)
    tm = x_ref.shape[0]
    t_in_seq = (i % tiles_per_seq) * tm

    @pl.when(i % tiles_per_seq == 0)
    def _():
        ext_ref[0:POOL_HALO, :] = jnp.zeros((POOL_HALO, ext_ref.shape[1]), jnp.float32)

    xb = x_ref[...].astype(jnp.bfloat16)
    u = _dot(xb, wu_ref[...])
    ext_ref[POOL_HALO:, :] = u
    ext = ext_ref[...]

    row = lax.broadcasted_iota(jnp.int32, (tm, 1), 0) + t_in_seq
    mixed = []
    for g, w in enumerate(POOL_WINDOWS):
        s = ext[:, g * LANES:(g + 1) * LANES]
        k = 1
        while k < w:
            s = s + pltpu.roll(s, k, 0)
            k *= 2
        count = jnp.minimum(row + 1, w).astype(jnp.float32)
        pooled = s[POOL_HALO:, :] / count - u[:, g * LANES:(g + 1) * LANES]
        mixed.append(_dot(pooled.astype(jnp.bfloat16), wpool_ref[g]))
    mixed = jnp.concatenate(mixed, axis=1) * pscale_ref[...]
    ext_ref[0:POOL_HALO, :] = u[tm - POOL_HALO:, :]

    y_b = _dot(mixed.astype(jnp.bfloat16), wb_ref[...])
    gate_b = _sigmoid(_dot(xb, wg_ref[...]) + bg_ref[...])
    o_ref[...] = (gate_b * y_b).astype(o_ref.dtype)


def _pool_branch(x2, w_u, w_gb, b_gb, w_pool, pool_scale, w_bb, seq, tm):
    T, D = x2.shape
    P = w_u.shape[1]
    kern = functools.partial(_pool_branch_kernel, tiles_per_seq=seq // tm)
    return pl.pallas_call(
        kern,
        grid=(T // tm,),
        in_specs=[pl.BlockSpec((tm, D), lambda i: (i, 0)), _full(w_u), _full(w_gb), _full(b_gb),
                  _full(w_pool), _full(pool_scale), _full(w_bb)],
        out_specs=pl.BlockSpec((tm, D), lambda i: (i, 0)),
        out_shape=jax.ShapeDtypeStruct((T, D), jnp.bfloat16),
        scratch_shapes=[pltpu.VMEM((tm + POOL_HALO, P), jnp.float32)],
        compiler_params=_params(("arbitrary",)),
        name="pool_branch",
    )(x2, w_u, w_gb, b_gb, w_pool, pool_scale, w_bb)


def _diff_attn_kernel(q_ref, k_ref, vt_ref, lam_ref, gain_ref, o_ref, acc_ref, m_ref, l_ref, *, lambda_init):
    qi = pl.program_id(2)
    qb = q_ref.shape[0]
    kt = ATTN_TILE

    q = q_ref[...]
    lane = lax.broadcasted_iota(jnp.int32, q.shape, 1)
    zero = jnp.zeros_like(q)
    qm = jnp.concatenate([jnp.where(lane < HEAD_DIM, q, zero), jnp.where(lane >= HEAD_DIM, q, zero)], axis=0)

    acc_ref[...] = jnp.zeros_like(acc_ref)
    m_ref[...] = jnp.full_like(m_ref, NEG_INF)
    l_ref[...] = jnp.zeros_like(l_ref)

    def step(j, masked):
        k_tile = k_ref[pl.ds(pl.multiple_of(j * kt, kt), kt), :]
        s = _dot_nt(k_tile, qm)
        if masked:
            key_chunk = lax.broadcasted_iota(jnp.int32, s.shape, 0) // CHUNK
            q_chunk = (lax.broadcasted_iota(jnp.int32, s.shape, 1) % qb) // CHUNK
            s = jnp.where(key_chunk <= q_chunk, s, NEG_INF)
        m_old = m_ref[...]
        m_new = jnp.maximum(m_old, jnp.max(s, axis=0, keepdims=True))
        alpha = jnp.exp(m_old - m_new)
        p = jnp.exp(s - m_new)
        l_ref[...] = alpha * l_ref[...] + jnp.sum(p, axis=0, keepdims=True)
        acc_ref[...] = alpha * acc_ref[...] + _dot(vt_ref[j], p.astype(jnp.bfloat16))
        m_ref[...] = m_new

    def full_step(j, carry):
        step(j, masked=False)
        return carry

    lax.fori_loop(0, qi, full_step, 0)
    step(qi, masked=True)

    lv = lam_ref[...]
    lam = (jnp.exp(jnp.sum(lv[0:1] * lv[1:2], axis=1, keepdims=True))
           - jnp.exp(jnp.sum(lv[2:3] * lv[3:4], axis=1, keepdims=True)) + lambda_init)
    acc = acc_ref[...]
    l = l_ref[...]
    o = acc[:, :qb] / l[:, :qb] - lam * (acc[:, qb:] / l[:, qb:])
    ms = jnp.mean(o * o, axis=0, keepdims=True)
    o = o * lax.rsqrt(ms + RMS_EPS) * gain_ref[...] * (1.0 - lambda_init)
    o_ref[...] = o.T.astype(o_ref.dtype)


def _diff_attn(qk, vT, lam_vecs, gain_col, batch, seq, lambda_init):
    T = qk.shape[0]
    n_heads = vT.shape[1]
    qb = ATTN_TILE
    nq = seq // qb
    kern = functools.partial(_diff_attn_kernel, lambda_init=lambda_init)
    return pl.pallas_call(
        kern,
        grid=(batch, n_heads, nq),
        in_specs=[
            pl.BlockSpec((qb, VALUE_DIM), lambda b, h, i: (b * nq + i, h)),
            pl.BlockSpec((seq, VALUE_DIM), lambda b, h, i: (b, n_heads + h)),
            pl.BlockSpec((None, None, seq // ATTN_TILE, VALUE_DIM, ATTN_TILE), lambda b, h, i: (b, h, 0, 0, 0)),
            _full(lam_vecs),
            _full(gain_col),
        ],
        out_specs=pl.BlockSpec((qb, VALUE_DIM), lambda b, h, i: (b * nq + i, h)),
        out_shape=jax.ShapeDtypeStruct((T, n_heads * VALUE_DIM), jnp.bfloat16),
        scratch_shapes=[
            pltpu.VMEM((VALUE_DIM, 2 * qb), jnp.float32),
            pltpu.VMEM((1, 2 * qb), jnp.float32),
            pltpu.VMEM((1, 2 * qb), jnp.float32),
        ],
        compiler_params=_params(("parallel", "parallel", "arbitrary")),
        name="diff_attn",
    )(qk, qk, vT, lam_vecs, gain_col)


def _split_bf16(a):
    hi = a.astype(jnp.bfloat16)
    lo = (a - hi.astype(jnp.float32)).astype(jnp.bfloat16)
    return hi, lo


def _merge_ln1_kernel(x_ref, o_ref, zb_ref, wa_ref, wg_ref, bg_ref, wout_ref, g1_ref, b1_ref,
                      wr_ref, br_ref, h_ref, idx_ref, wts_ref):
    x = x_ref[...]
    xb = x.astype(jnp.bfloat16)
    gate_a = _sigmoid(_dot(xb, wg_ref[...]) + bg_ref[...])
    y_a = _dot(o_ref[...], wa_ref[...])
    merged = gate_a * y_a + zb_ref[...].astype(jnp.float32)
    mix = _dot(merged.astype(jnp.bfloat16), wout_ref[...])
    h = _layer_norm(ALPHA * x + mix, g1_ref[...], b1_ref[...])
    h_ref[...] = h

    h_hi, h_lo = _split_bf16(h)
    w_hi, w_lo = _split_bf16(wr_ref[...])
    logits = _dot_nt(w_hi, h_hi) + _dot_nt(w_hi, h_lo) + _dot_nt(w_lo, h_hi) + br_ref[...]

    tm = x.shape[0]
    e_logit = logits[:N_EXPERTS]
    g_logit = logits[N_EXPERTS:N_EXPERTS + N_GROUPS]
    g_row = lax.broadcasted_iota(jnp.int32, g_logit.shape, 0)
    g_max = jnp.max(g_logit, axis=0, keepdims=True)
    g_idx = jnp.min(jnp.where(g_logit == g_max, g_row, N_GROUPS), axis=0, keepdims=True)
    g_p = 1.0 / jnp.sum(jnp.exp(g_logit - g_max), axis=0, keepdims=True)

    sel = jnp.zeros((EXPERTS_PER_GROUP, tm), jnp.float32)
    for g in range(N_GROUPS):
        sel = jnp.where(g_idx == g, e_logit[g * EXPERTS_PER_GROUP:(g + 1) * EXPERTS_PER_GROUP], sel)
    e_row = lax.broadcasted_iota(jnp.int32, sel.shape, 0)
    m1 = jnp.max(sel, axis=0, keepdims=True)
    i1 = jnp.min(jnp.where(sel == m1, e_row, EXPERTS_PER_GROUP), axis=0, keepdims=True)
    rest = jnp.where(e_row == i1, -jnp.inf, sel)
    m2 = jnp.max(rest, axis=0, keepdims=True)
    i2 = jnp.min(jnp.where(rest == m2, e_row, EXPERTS_PER_GROUP), axis=0, keepdims=True)
    p2 = jnp.exp(m2 - m1)
    w1 = g_p / (1.0 + p2)
    w2 = g_p * p2 / (1.0 + p2)

    base = g_idx * EXPERTS_PER_GROUP
    zi = jnp.zeros((ROUTE_ROWS - TOP_K, tm), jnp.int32)
    zf = jnp.zeros((ROUTE_ROWS - TOP_K, tm), jnp.float32)
    idx_ref[...] = jnp.concatenate([base + i1, base + i2, zi], axis=0)
    wts_ref[...] = jnp.concatenate([w1, w2, zf], axis=0)


def _merge_ln1(x2, o, zb, w_a, w_ga, b_ga, w_out, g1, b1, w_rT, b_r):
    T, D = x2.shape
    tm = ROUTE_TILE
    row = pl.BlockSpec((tm, D), lambda i: (i, 0))
    route = pl.BlockSpec((None, ROUTE_ROWS, tm), lambda i: (i, 0, 0))
    return pl.pallas_call(
        _merge_ln1_kernel,
        grid=(T // tm,),
        in_specs=[row, row, row, _full(w_a), _full(w_ga), _full(b_ga), _full(w_out), _full(g1), _full(b1),
                  _full(w_rT), _full(b_r)],
        out_specs=[row, route, route],
        out_shape=[jax.ShapeDtypeStruct((T, D), jnp.float32),
                   jax.ShapeDtypeStruct((T // tm, ROUTE_ROWS, tm), jnp.int32),
                   jax.ShapeDtypeStruct((T // tm, ROUTE_ROWS, tm), jnp.float32)],
        compiler_params=_params(("parallel",)),
        name="merge_ln1",
    )(x2, o, zb, w_a, w_ga, b_ga, w_out, g1, b1, w_rT, b_r)


def _route_plan_kernel(idx_ref, dest_ref, be_ref, nused_ref, rank_ref, run_ref):
    n_tiles = idx_ref.shape[0]
    c = ROUTE_TILE
    e_col = lax.broadcasted_iota(jnp.int32, (N_EXPERTS, c), 0)
    upper = (lax.broadcasted_iota(jnp.int32, (c, c), 0) < lax.broadcasted_iota(jnp.int32, (c, c), 1)
             ).astype(jnp.bfloat16)
    run_ref[...] = jnp.zeros_like(run_ref)

    def rank_tile(ci, carry):
        idx = idx_ref[ci]
        oh0 = (e_col == idx[0:1]).astype(jnp.float32)
        oh1 = (e_col == idx[1:2]).astype(jnp.float32)
        both = oh0 + oh1
        run = run_ref[...]
        before = _dot(both.astype(jnp.bfloat16), upper) + run[:, 0:1]
        r0 = jnp.sum(oh0 * before, axis=0, keepdims=True).astype(jnp.int32)
        r1 = jnp.sum(oh1 * before, axis=0, keepdims=True).astype(jnp.int32)
        rank_ref[ci] = jnp.concatenate([r0, r1, jnp.zeros((ROUTE_ROWS - TOP_K, c), jnp.int32)], axis=0)
        run_ref[...] = run + jnp.sum(both, axis=1, keepdims=True)
        return carry

    lax.fori_loop(0, n_tiles, rank_tile, 0)

    counts = run_ref[...]
    padded = jnp.floor((counts + (MOE_BLOCK - 1)) * (1.0 / MOE_BLOCK)) * MOE_BLOCK
    e_row = lax.broadcasted_iota(jnp.int32, padded.shape, 0)
    ends = padded
    k = 1
    while k < N_EXPERTS:
        ends = ends + jnp.where(e_row >= k, pltpu.roll(ends, k, 0), 0.0)
        k *= 2
    start = (ends - padded)[:, 0:1]

    def dest_tile(ci, carry):
        idx = idx_ref[ci]
        rank = rank_ref[ci]
        rows = []
        for k in range(TOP_K):
            oh = (e_col == idx[k:k + 1]).astype(jnp.float32)
            seg = jnp.sum(oh * start, axis=0, keepdims=True).astype(jnp.int32)
            rows.append(seg + rank[k:k + 1])
        rows.append(jnp.zeros((ROUTE_ROWS - TOP_K, c), jnp.int32))
        dest_ref[ci] = jnp.concatenate(rows, axis=0)
        return carry

    lax.fori_loop(0, n_tiles, dest_tile, 0)

    blk = (lax.broadcasted_iota(jnp.int32, (N_EXPERTS, be_ref.shape[1]), 1) * MOE_BLOCK).astype(jnp.float32)
    be = jnp.sum((ends[:, 0:1] <= blk).astype(jnp.int32), axis=0, keepdims=True)
    be_ref[...] = jnp.minimum(be, N_EXPERTS - 1)
    total = ends[N_EXPERTS - 1:N_EXPERTS, :] * (1.0 / MOE_BLOCK)
    nused_ref[...] = total.astype(jnp.int32)


def _route_plan(idx, n_blocks):
    nb_pad = -(-n_blocks // LANES) * LANES
    return pl.pallas_call(
        _route_plan_kernel,
        grid=(1,),
        in_specs=[_full(idx)],
        out_specs=[pl.BlockSpec(idx.shape, lambda i: (0, 0, 0)),
                   pl.BlockSpec((1, nb_pad), lambda i: (0, 0)),
                   pl.BlockSpec((1, LANES), lambda i: (0, 0))],
        out_shape=[jax.ShapeDtypeStruct(idx.shape, jnp.int32),
                   jax.ShapeDtypeStruct((1, nb_pad), jnp.int32),
                   jax.ShapeDtypeStruct((1, LANES), jnp.int32)],
        scratch_shapes=[pltpu.VMEM(idx.shape, jnp.int32),
                        pltpu.VMEM((N_EXPERTS, LANES), jnp.float32)],
        compiler_params=_params(("arbitrary",)),
        name="route_plan",
    )(idx)


def _route_smem_spec():
    return pl.BlockSpec((None, ROUTE_ROWS, ROUTE_TILE), lambda i: (i, 0, 0), memory_space=pltpu.SMEM)


def _dispatch_kernel(dest_ref, h_hbm, xs_in_hbm, xs_hbm, sem):
    del xs_in_hbm
    base = pl.program_id(0) * ROUTE_TILE

    def row_copy(src_row, dst_row):
        return pltpu.make_async_copy(h_hbm.at[pl.ds(src_row, 1)], xs_hbm.at[pl.ds(dst_row, 1)], sem)

    def start(t, carry):
        for k in range(TOP_K):
            row_copy(base + t, dest_ref[k, t]).start()
        return carry

    def wait(t, carry):
        for k in range(TOP_K):
            row_copy(0, 0).wait()
        return carry

    lax.fori_loop(0, ROUTE_TILE, start, 0)
    lax.fori_loop(0, ROUTE_TILE, wait, 0)


def _dispatch(dest, h, n_rows):
    T, D = h.shape
    xs_zero = jnp.zeros((n_rows, D), h.dtype)
    return pl.pallas_call(
        _dispatch_kernel,
        grid=(T // ROUTE_TILE,),
        in_specs=[_route_smem_spec(),
                  pl.BlockSpec(memory_space=pl.ANY),
                  pl.BlockSpec(memory_space=pl.ANY)],
        out_specs=pl.BlockSpec(memory_space=pl.ANY),
        out_shape=jax.ShapeDtypeStruct((n_rows, D), h.dtype),
        scratch_shapes=[pltpu.SemaphoreType.DMA(())],
        input_output_aliases={2: 0},
        compiler_params=_params(("arbitrary",)),
        name="dispatch",
    )(dest, h, xs_zero)


def _moe_ffn_kernel(be_ref, nused_ref, xs_ref, wgu_ref, wdn_ref, ys_ref):
    del be_ref
    b = pl.program_id(0)
    ff = wdn_ref.shape[0]

    @pl.when(b < nused_ref[0])
    def _():
        xb = xs_ref[...].astype(jnp.bfloat16)
        gu = _dot(xb, wgu_ref[...])
        gate = gu[:, :ff]
        up = gu[:, ff:]
        hidden = gate * _sigmoid(gate) * up
        ys_ref[...] = _dot(hidden.astype(jnp.bfloat16), wdn_ref[...])

    @pl.when(b >= nused_ref[0])
    def _():
        ys_ref[...] = jnp.zeros_like(ys_ref)


def _moe_ffn(block_expert, n_used, xs, w_gu, w_dn):
    P, D = xs.shape
    n_blocks = P // MOE_BLOCK
    ff2 = w_gu.shape[2]
    ff = w_dn.shape[1]

    def row_map(b, be, nu):
        return (jnp.minimum(b, nu[0] - 1), 0)

    def w_map(b, be, nu):
        return (be[jnp.minimum(b, nu[0] - 1)], 0, 0)

    grid_spec = pltpu.PrefetchScalarGridSpec(
        num_scalar_prefetch=2,
        grid=(n_blocks,),
        in_specs=[pl.BlockSpec((MOE_BLOCK, D), row_map),
                  pl.BlockSpec((None, D, ff2), w_map),
                  pl.BlockSpec((None, ff, D), w_map)],
        out_specs=pl.BlockSpec((MOE_BLOCK, D), lambda b, be, nu: (b, 0)),
    )
    return pl.pallas_call(
        _moe_ffn_kernel,
        grid_spec=grid_spec,
        out_shape=jax.ShapeDtypeStruct((P, D), jnp.float32),
        compiler_params=_params(("arbitrary",)),
        name="moe_ffn",
    )(block_expert, n_used, xs, w_gu, w_dn)


def _combine_ln2_kernel(dest_ref, h_ref, wts_ref, g2_ref, b2_ref, ys_hbm, o_ref, buf_ref, sem):
    def row_copy(src_row, k, dst_row):
        return pltpu.make_async_copy(ys_hbm.at[pl.ds(src_row, 1)], buf_ref.at[k, pl.ds(dst_row, 1)], sem)

    def start(t, carry):
        for k in range(TOP_K):
            row_copy(dest_ref[k, t], k, t).start()
        return carry

    def wait(t, carry):
        for k in range(TOP_K):
            row_copy(0, k, 0).wait()
        return carry

    lax.fori_loop(0, ROUTE_TILE, start, 0)
    lax.fori_loop(0, ROUTE_TILE, wait, 0)

    w = wts_ref[...]
    ffn = w[:, 0:1] * buf_ref[0] + w[:, 1:2] * buf_ref[1]
    o_ref[...] = _layer_norm(ALPHA * h_ref[...] + ffn, g2_ref[...], b2_ref[...])


def _combine_ln2(dest, h, wts_t, g2, b2, ys):
    T, D = h.shape
    tc = ROUTE_TILE
    return pl.pallas_call(
        _combine_ln2_kernel,
        grid=(T // tc,),
        in_specs=[_route_smem_spec(),
                  pl.BlockSpec((tc, D), lambda i: (i, 0)),
                  pl.BlockSpec((tc, ROUTE_ROWS), lambda i: (i, 0)),
                  _full(g2), _full(b2),
                  pl.BlockSpec(memory_space=pl.ANY)],
        out_specs=pl.BlockSpec((tc, D), lambda i: (i, 0)),
        out_shape=jax.ShapeDtypeStruct((T, D), jnp.float32),
        scratch_shapes=[pltpu.VMEM((TOP_K, tc, D), jnp.float32), pltpu.SemaphoreType.DMA(())],
        compiler_params=_params(("arbitrary",)),
        name="combine_ln2",
    )(dest, h, wts_t, g2, b2, ys)


def _layer(x, positions, w_in, gate_bias, lam_vecs, subln_gain, w_pool, pool_scale, w_branch_a, w_branch_b,
           w_out, ln1_gain, ln1_bias, w_router_group, b_router_group, w_router_expert, b_router_expert,
           w_gate_up, w_down, ln2_gain, ln2_bias, lambda_init):
    B, S, D = x.shape
    T = B * S
    attn_w = w_branch_a.shape[0]
    pool_w = w_branch_b.shape[0]
    assert D == attn_w and pool_w == len(POOL_WINDOWS) * LANES and S % ATTN_TILE == 0 and T % ROUTE_TILE == 0
    bf = jnp.bfloat16
    row = lambda a: a.reshape(1, -1).astype(jnp.float32)

    x2 = x.reshape(T, D)
    pos2 = positions.reshape(T, 1)
    c_k, c_v, c_u = 2 * attn_w, 3 * attn_w, 3 * attn_w + pool_w
    w_qk = w_in[:, :c_k].astype(bf)
    w_vT = w_in[:, c_k:c_v].T.astype(bf)
    w_u = w_in[:, c_v:c_u].astype(bf)
    w_ga = w_in[:, c_u:c_u + D].astype(bf)
    w_gb = w_in[:, c_u + D:].astype(bf)
    b_ga = row(gate_bias[:D])
    b_gb = row(gate_bias[D:])
    half = HEAD_DIM // 2
    inv_freq = ROPE_THETA ** (-jnp.arange(half, dtype=jnp.float32) * (2.0 / HEAD_DIM))
    inv_freq = jnp.tile(inv_freq, LANES // half).reshape(1, LANES)

    tm = min(512, S)
    qk = _qk_proj(x2, pos2, inv_freq, w_qk, tm, tn=512)
    vT = _v_proj(x2, w_vT, B, S, ATTN_TILE)
    zb = _pool_branch(x2, w_u, w_gb, b_gb, w_pool.astype(bf), row(pool_scale), w_branch_b.astype(bf), S, tm)
    o = _diff_attn(qk, vT, lam_vecs.astype(jnp.float32), subln_gain.reshape(-1, 1).astype(jnp.float32),
                   B, S, lambda_init)

    n_route = N_EXPERTS + N_GROUPS
    n_route_pad = -(-n_route // SUBLANES) * SUBLANES
    w_rT = jnp.concatenate([w_router_expert, w_router_group], axis=1).T.astype(jnp.float32)
    w_rT = jnp.pad(w_rT, ((0, n_route_pad - n_route), (0, 0)))
    b_r = jnp.pad(jnp.concatenate([b_router_expert, b_router_group]).astype(jnp.float32),
                  (0, n_route_pad - n_route)).reshape(-1, 1)
    h, idx, wts = _merge_ln1(x2, o, zb, w_branch_a.astype(bf), w_ga, b_ga, w_out.astype(bf),
                             row(ln1_gain), row(ln1_bias), w_rT, b_r)

    n_blocks = -(-(T * TOP_K) // MOE_BLOCK) + N_EXPERTS
    dest, block_expert, n_used = _route_plan(idx, n_blocks)
    xs = _dispatch(dest, h, n_blocks * MOE_BLOCK)
    ys = _moe_ffn(block_expert[0, :n_blocks], n_used[0, :1], xs, w_gate_up.astype(bf), w_down.astype(bf))
    wts_t = wts.transpose(0, 2, 1).reshape(T, ROUTE_ROWS)
    out = _combine_ln2(dest, h, wts_t, row(ln2_gain), row(ln2_bias), ys)
    return out.reshape(B, S, D)


def kernel(x, positions, w_in, gate_bias, lam_vecs, subln_gain, w_pool, pool_scale, w_branch_a, w_branch_b,
           w_out, ln1_gain, ln1_bias, w_router_group, b_router_group, w_router_expert, b_router_expert,
           w_gate_up, w_down, ln2_gain, ln2_bias):
    for layer in range(w_in.shape[0]):
        lambda_init = 0.8 - 0.6 * math.exp(-0.3 * layer)
        x = _layer(x, positions, w_in[layer], gate_bias[layer], lam_vecs[layer], subln_gain[layer],
                   w_pool[layer], pool_scale[layer], w_branch_a[layer], w_branch_b[layer], w_out[layer],
                   ln1_gain[layer], ln1_bias[layer], w_router_group[layer], b_router_group[layer],
                   w_router_expert[layer], b_router_expert[layer], w_gate_up[layer], w_down[layer],
                   ln2_gain[layer], ln2_bias[layer], lambda_init)
    return x
```

```python
import functools
import math

import jax
import jax.numpy as jnp
from jax import lax
from jax.experimental import pallas as pl
from jax.experimental.pallas import tpu as pltpu

HEAD_DIM = 64
VALUE_DIM = 2 * HEAD_DIM
CHUNK = 64
ROPE_THETA = 10000.0
POOL_WINDOWS = (2, 4, 8, 16)
POOL_HALO = 16
N_GROUPS = 4
EXPERTS_PER_GROUP = 8
N_EXPERTS = N_GROUPS * EXPERTS_PER_GROUP
TOP_K = 2
MOE_BLOCK = 256
LN_EPS = 1e-5
RMS_EPS = 1e-5
DEPTH = 1
ALPHA = (2.0 * DEPTH) ** 0.25
NEG_INF = -1e30

LANES = 128
SUBLANES = 8
VMEM_LIMIT_BYTES = 56 * 1024 * 1024

ATTN_TILE = 256
HEADS_PER_PAIR = 2
PAIR_WIDTH = HEADS_PER_PAIR * VALUE_DIM
ROPE_HALF = HEAD_DIM // 2
LOG2_E = math.log2(math.e)
ROUTE_ROWS = 8
ROUTE_TILE = 256
ISSUE_UNROLL = 8


def _params(semantics):
    return pltpu.CompilerParams(dimension_semantics=semantics, vmem_limit_bytes=VMEM_LIMIT_BYTES)


def _full(a):
    return pl.BlockSpec(a.shape, lambda *_: (0,) * a.ndim)


def _dot(a, b):
    return jnp.dot(a, b, preferred_element_type=jnp.float32)


def _dot_nt(a, b):
    return lax.dot_general(a, b, (((1,), (1,)), ((), ())), preferred_element_type=jnp.float32)


def _sigmoid(x):
    return 1.0 / (1.0 + jnp.exp(-x))


def _layer_norm(r, gain, bias):
    mu = jnp.mean(r, axis=-1, keepdims=True)
    c = r - mu
    var = jnp.mean(c * c, axis=-1, keepdims=True)
    return c * lax.rsqrt(var + LN_EPS) * gain + bias


def _qk_proj_kernel(x_ref, pos_ref, invf_ref, w_ref, o_ref, xb_ref, cos_ref, sin_ref, *, n_q_tiles):
    j = pl.program_id(1)

    @pl.when(j == 0)
    def _():
        xb_ref[...] = x_ref[...].astype(jnp.bfloat16)
        ang = pos_ref[...].astype(jnp.float32) * invf_ref[...]
        cos_ref[...] = jnp.cos(ang)
        sin_ref[...] = jnp.sin(ang)

    scale = jnp.where(j < n_q_tiles, HEAD_DIM ** -0.5 * LOG2_E, 1.0)
    cos = cos_ref[...] * scale
    sin = sin_ref[...] * scale
    for c in range(w_ref.shape[1] // PAIR_WIDTH):
        acc = _dot(xb_ref[...], w_ref[:, c * PAIR_WIDTH:(c + 1) * PAIR_WIDTH])
        t1 = acc[:, :LANES]
        t2 = acc[:, LANES:]
        o_ref[:, c * PAIR_WIDTH:c * PAIR_WIDTH + LANES] = (t1 * cos - t2 * sin).astype(o_ref.dtype)
        o_ref[:, c * PAIR_WIDTH + LANES:(c + 1) * PAIR_WIDTH] = (t2 * cos + t1 * sin).astype(o_ref.dtype)


def _qk_proj(x2, pos2, inv_freq, w_qk, tm, tn):
    T, D = x2.shape
    n_cols = w_qk.shape[1]
    kern = functools.partial(_qk_proj_kernel, n_q_tiles=(n_cols // 2) // tn)
    return pl.pallas_call(
        kern,
        grid=(T // tm, n_cols // tn),
        in_specs=[
            pl.BlockSpec((tm, D), lambda i, j: (i, 0)),
            pl.BlockSpec((tm, 1), lambda i, j: (i, 0)),
            pl.BlockSpec((1, LANES), lambda i, j: (0, 0)),
            pl.BlockSpec((D, tn), lambda i, j: (0, j)),
        ],
        out_specs=pl.BlockSpec((tm, tn), lambda i, j: (i, j)),
        out_shape=jax.ShapeDtypeStruct((T, n_cols), jnp.bfloat16),
        scratch_shapes=[
            pltpu.VMEM((tm, D), jnp.bfloat16),
            pltpu.VMEM((tm, LANES), jnp.float32),
            pltpu.VMEM((tm, LANES), jnp.float32),
        ],
        compiler_params=_params(("parallel", "arbitrary")),
        name="qk_proj",
    )(x2, pos2, inv_freq, w_qk)


def _v_proj_kernel(x_ref, w_ref, o_ref):
    xb = x_ref[...].astype(jnp.bfloat16)
    vt = _dot_nt(w_ref[...], xb)
    o_ref[...] = vt.reshape(o_ref.shape).astype(o_ref.dtype)


def _v_proj(x2, w_vT, batch, seq, kt):
    T, D = x2.shape
    n_heads = w_vT.shape[0] // VALUE_DIM
    tiles_per_seq = seq // kt
    return pl.pallas_call(
        _v_proj_kernel,
        grid=(T // kt,),
        in_specs=[pl.BlockSpec((kt, D), lambda i: (i, 0)), _full(w_vT)],
        out_specs=pl.BlockSpec((None, n_heads, None, VALUE_DIM, kt),
                               lambda i: (i // tiles_per_seq, 0, i % tiles_per_seq, 0, 0)),
        out_shape=jax.ShapeDtypeStruct((batch, n_heads, tiles_per_seq, VALUE_DIM, kt), jnp.bfloat16),
        compiler_params=_params(("parallel",)),
        name="v_proj",
    )(x2, w_vT)


def _pool_branch_kernel(x_ref, wu_ref, wg_ref, bg_ref, wpool_ref, pscale_ref, wb_ref, o_ref, ext_ref,
                        *, tiles_per_seq):
    i = pl.program_id(0)
    tm = x_ref.shape[0]
    t_in_seq = (i % tiles_per_seq) * tm

    @pl.when(i % tiles_per_seq == 0)
    def _():
        ext_ref[0:POOL_HALO, :] = jnp.zeros((POOL_HALO, ext_ref.shape[1]), jnp.float32)

    xb = x_ref[...].astype(jnp.bfloat16)
    u = _dot(xb, wu_ref[...])
    ext_ref[POOL_HALO:, :] = u
    ext = ext_ref[...]

    row = lax.broadcasted_iota(jnp.int32, (tm, 1), 0) + t_in_seq
    mixed = []
    for g, w in enumerate(POOL_WINDOWS):
        s = ext[:, g * LANES:(g + 1) * LANES]
        k = 1
        while k < w:
            s = s + pltpu.roll(s, k, 0)
            k *= 2
        count = jnp.minimum(row + 1, w).astype(jnp.float32)
        pooled = s[POOL_HALO:, :] / count - u[:, g * LANES:(g + 1) * LANES]
        mixed.append(_dot(pooled.astype(jnp.bfloat16), wpool_ref[g]))
    mixed = jnp.concatenate(mixed, axis=1) * pscale_ref[...]
    ext_ref[0:POOL_HALO, :] = u[tm - POOL_HALO:, :]

    y_b = _dot(mixed.astype(jnp.bfloat16), wb_ref[...])
    gate_b = _sigmoid(_dot(xb, wg_ref[...]) + bg_ref[...])
    o_ref[...] = (gate_b * y_b).astype(o_ref.dtype)


def _pool_branch(x2, w_u, w_gb, b_gb, w_pool, pool_scale, w_bb, seq, tm):
    T, D = x2.shape
    P = w_u.shape[1]
    kern = functools.partial(_pool_branch_kernel, tiles_per_seq=seq // tm)
    return pl.pallas_call(
        kern,
        grid=(T // tm,),
        in_specs=[pl.BlockSpec((tm, D), lambda i: (i, 0)), _full(w_u), _full(w_gb), _full(b_gb),
                  _full(w_pool), _full(pool_scale), _full(w_bb)],
        out_specs=pl.BlockSpec((tm, D), lambda i: (i, 0)),
        out_shape=jax.ShapeDtypeStruct((T, D), jnp.bfloat16),
        scratch_shapes=[pltpu.VMEM((tm + POOL_HALO, P), jnp.float32)],
        compiler_params=_params(("arbitrary",)),
        name="pool_branch",
    )(x2, w_u, w_gb, b_gb, w_pool, pool_scale, w_bb)


def _diff_attn_kernel(q_ref, k_ref, vt_ref, lam_ref, gain_ref, o_ref, *, lambda_init):
    qb = kt = ATTN_TILE
    n_maps = 2 * HEADS_PER_PAIR
    nq = q_ref.shape[0] // qb

    lv = lam_ref[...]
    lam = (jnp.exp(jnp.sum(lv[0:1] * lv[1:2], axis=1, keepdims=True))
           - jnp.exp(jnp.sum(lv[2:3] * lv[3:4], axis=1, keepdims=True)) + lambda_init)
    gain = gain_ref[...] * (1.0 - lambda_init)

    slot = (lax.broadcasted_iota(jnp.int32, (qb, PAIR_WIDTH), 1) % LANES) // ROPE_HALF
    key_chunk = lax.broadcasted_iota(jnp.int32, (kt, n_maps * qb), 0) // CHUNK
    q_chunk = (lax.broadcasted_iota(jnp.int32, (kt, n_maps * qb), 1) % qb) // CHUNK
    allowed = key_chunk <= q_chunk

    for qi in range(nq):
        q = q_ref[qi * qb:(qi + 1) * qb, :]
        zero = jnp.zeros_like(q)
        qm = jnp.concatenate([jnp.where(slot == mp, q, zero) for mp in range(n_maps)], axis=0)
        m = l = None
        acc = [None] * HEADS_PER_PAIR
        for j in range(qi + 1):
            s = _dot_nt(k_ref[j * kt:(j + 1) * kt, :], qm)
            if j == qi:
                s = jnp.where(allowed, s, NEG_INF)
            s_max = jnp.max(s, axis=0, keepdims=True)
            m_new = s_max if j == 0 else jnp.maximum(m, s_max)
            p = jnp.exp2(s - m_new)
            p_sum = jnp.sum(p, axis=0, keepdims=True)
            pb = p.astype(jnp.bfloat16)
            if j == 0:
                l = p_sum
                for hh in range(HEADS_PER_PAIR):
                    acc[hh] = _dot(vt_ref[hh, j], pb[:, 2 * hh * qb:2 * (hh + 1) * qb])
            else:
                alpha = jnp.exp2(m - m_new)
                l = alpha * l + p_sum
                for hh in range(HEADS_PER_PAIR):
                    cols = slice(2 * hh * qb, 2 * (hh + 1) * qb)
                    acc[hh] = alpha[:, cols] * acc[hh] + _dot(vt_ref[hh, j], pb[:, cols])
            m = m_new

        for hh in range(HEADS_PER_PAIR):
            c0 = 2 * hh * qb
            o = (acc[hh][:, :qb] / l[:, c0:c0 + qb]
                 - lam * (acc[hh][:, qb:] / l[:, c0 + qb:c0 + 2 * qb]))
            ms = jnp.mean(o * o, axis=0, keepdims=True)
            o = o * lax.rsqrt(ms + RMS_EPS) * gain
            o_ref[qi * qb:(qi + 1) * qb, hh * VALUE_DIM:(hh + 1) * VALUE_DIM] = o.T.astype(o_ref.dtype)


def _diff_attn(qk, vT, lam_vecs, gain_col, batch, seq, lambda_init):
    T = qk.shape[0]
    n_heads = vT.shape[1]
    n_pairs = n_heads // HEADS_PER_PAIR
    kern = functools.partial(_diff_attn_kernel, lambda_init=lambda_init)
    return pl.pallas_call(
        kern,
        grid=(batch, n_pairs),
        in_specs=[
            pl.BlockSpec((seq, PAIR_WIDTH), lambda b, g: (b, g)),
            pl.BlockSpec((seq, PAIR_WIDTH), lambda b, g: (b, n_pairs + g)),
            pl.BlockSpec((None, HEADS_PER_PAIR, seq // ATTN_TILE, VALUE_DIM, ATTN_TILE),
                         lambda b, g: (b, g, 0, 0, 0)),
            _full(lam_vecs),
            _full(gain_col),
        ],
        out_specs=pl.BlockSpec((seq, PAIR_WIDTH), lambda b, g: (b, g)),
        out_shape=jax.ShapeDtypeStruct((T, n_heads * VALUE_DIM), jnp.bfloat16),
        compiler_params=_params(("parallel", "parallel")),
        name="diff_attn",
    )(qk, qk, vT, lam_vecs, gain_col)


def _split_bf16(a):
    hi = a.astype(jnp.bfloat16)
    lo = (a - hi.astype(jnp.float32)).astype(jnp.bfloat16)
    return hi, lo


def _merge_ln1_kernel(x_ref, o_ref, zb_ref, wa_ref, wg_ref, bg_ref, wout_ref, g1_ref, b1_ref,
                      wr_ref, br_ref, h_ref, idx_ref, wts_ref):
    x = x_ref[...]
    xb = x.astype(jnp.bfloat16)
    gate_a = _sigmoid(_dot(xb, wg_ref[...]) + bg_ref[...])
    y_a = _dot(o_ref[...], wa_ref[...])
    merged = gate_a * y_a + zb_ref[...].astype(jnp.float32)
    mix = _dot(merged.astype(jnp.bfloat16), wout_ref[...])
    h = _layer_norm(ALPHA * x + mix, g1_ref[...], b1_ref[...])
    h_ref[...] = h

    h_hi, h_lo = _split_bf16(h)
    w_hi, w_lo = _split_bf16(wr_ref[...])
    logits = _dot_nt(w_hi, h_hi) + _dot_nt(w_hi, h_lo) + _dot_nt(w_lo, h_hi) + br_ref[...]

    tm = x.shape[0]
    e_logit = logits[:N_EXPERTS]
    g_logit = logits[N_EXPERTS:N_EXPERTS + N_GROUPS]
    g_row = lax.broadcasted_iota(jnp.int32, g_logit.shape, 0)
    g_max = jnp.max(g_logit, axis=0, keepdims=True)
    g_idx = jnp.min(jnp.where(g_logit == g_max, g_row, N_GROUPS), axis=0, keepdims=True)
    g_p = 1.0 / jnp.sum(jnp.exp(g_logit - g_max), axis=0, keepdims=True)

    sel = jnp.zeros((EXPERTS_PER_GROUP, tm), jnp.float32)
    for g in range(N_GROUPS):
        sel = jnp.where(g_idx == g, e_logit[g * EXPERTS_PER_GROUP:(g + 1) * EXPERTS_PER_GROUP], sel)
    e_row = lax.broadcasted_iota(jnp.int32, sel.shape, 0)
    m1 = jnp.max(sel, axis=0, keepdims=True)
    i1 = jnp.min(jnp.where(sel == m1, e_row, EXPERTS_PER_GROUP), axis=0, keepdims=True)
    rest = jnp.where(e_row == i1, -jnp.inf, sel)
    m2 = jnp.max(rest, axis=0, keepdims=True)
    i2 = jnp.min(jnp.where(rest == m2, e_row, EXPERTS_PER_GROUP), axis=0, keepdims=True)
    p2 = jnp.exp(m2 - m1)
    w1 = g_p / (1.0 + p2)
    w2 = g_p * p2 / (1.0 + p2)

    base = g_idx * EXPERTS_PER_GROUP
    zi = jnp.zeros((ROUTE_ROWS - TOP_K, tm), jnp.int32)
    zf = jnp.zeros((ROUTE_ROWS - TOP_K, tm), jnp.float32)
    idx_ref[...] = jnp.concatenate([base + i1, base + i2, zi], axis=0)
    wts_ref[...] = jnp.concatenate([w1, w2, zf], axis=0)


def _merge_ln1(x2, o, zb, w_a, w_ga, b_ga, w_out, g1, b1, w_rT, b_r):
    T, D = x2.shape
    tm = ROUTE_TILE
    row = pl.BlockSpec((tm, D), lambda i: (i, 0))
    route = pl.BlockSpec((None, ROUTE_ROWS, tm), lambda i: (i, 0, 0))
    return pl.pallas_call(
        _merge_ln1_kernel,
        grid=(T // tm,),
        in_specs=[row, row, row, _full(w_a), _full(w_ga), _full(b_ga), _full(w_out), _full(g1), _full(b1),
                  _full(w_rT), _full(b_r)],
        out_specs=[row, route, route],
        out_shape=[jax.ShapeDtypeStruct((T, D), jnp.float32),
                   jax.ShapeDtypeStruct((T // tm, ROUTE_ROWS, tm), jnp.int32),
                   jax.ShapeDtypeStruct((T // tm, ROUTE_ROWS, tm), jnp.float32)],
        compiler_params=_params(("parallel",)),
        name="merge_ln1",
    )(x2, o, zb, w_a, w_ga, b_ga, w_out, g1, b1, w_rT, b_r)


def _route_plan_kernel(idx_ref, dest_ref, be_ref, nused_ref, rank_ref, run_ref):
    n_tiles = idx_ref.shape[0]
    c = ROUTE_TILE
    e_col = lax.broadcasted_iota(jnp.int32, (N_EXPERTS, c), 0)
    upper = (lax.broadcasted_iota(jnp.int32, (c, c), 0) < lax.broadcasted_iota(jnp.int32, (c, c), 1)
             ).astype(jnp.bfloat16)
    run_ref[...] = jnp.zeros_like(run_ref)

    def rank_tile(ci, carry):
        idx = idx_ref[ci]
        oh0 = (e_col == idx[0:1]).astype(jnp.float32)
        oh1 = (e_col == idx[1:2]).astype(jnp.float32)
        both = oh0 + oh1
        run = run_ref[...]
        before = _dot(both.astype(jnp.bfloat16), upper) + run[:, 0:1]
        r0 = jnp.sum(oh0 * before, axis=0, keepdims=True).astype(jnp.int32)
        r1 = jnp.sum(oh1 * before, axis=0, keepdims=True).astype(jnp.int32)
        rank_ref[ci] = jnp.concatenate([r0, r1, jnp.zeros((ROUTE_ROWS - TOP_K, c), jnp.int32)], axis=0)
        run_ref[...] = run + jnp.sum(both, axis=1, keepdims=True)
        return carry

    lax.fori_loop(0, n_tiles, rank_tile, 0)

    counts = run_ref[...]
    padded = jnp.floor((counts + (MOE_BLOCK - 1)) * (1.0 / MOE_BLOCK)) * MOE_BLOCK
    e_row = lax.broadcasted_iota(jnp.int32, padded.shape, 0)
    ends = padded
    k = 1
    while k < N_EXPERTS:
        ends = ends + jnp.where(e_row >= k, pltpu.roll(ends, k, 0), 0.0)
        k *= 2
    start = (ends - padded)[:, 0:1]

    def dest_tile(ci, carry):
        idx = idx_ref[ci]
        rank = rank_ref[ci]
        rows = []
        for k in range(TOP_K):
            oh = (e_col == idx[k:k + 1]).astype(jnp.float32)
            seg = jnp.sum(oh * start, axis=0, keepdims=True).astype(jnp.int32)
            rows.append(seg + rank[k:k + 1])
        rows.append(jnp.zeros((ROUTE_ROWS - TOP_K, c), jnp.int32))
        dest_ref[ci] = jnp.concatenate(rows, axis=0)
        return carry

    lax.fori_loop(0, n_tiles, dest_tile, 0)

    blk = (lax.broadcasted_iota(jnp.int32, (N_EXPERTS, be_ref.shape[1]), 1) * MOE_BLOCK).astype(jnp.float32)
    be = jnp.sum((ends[:, 0:1] <= blk).astype(jnp.int32), axis=0, keepdims=True)
    be_ref[...] = jnp.minimum(be, N_EXPERTS - 1)
    total = ends[N_EXPERTS - 1:N_EXPERTS, :] * (1.0 / MOE_BLOCK)
    nused_ref[...] = total.astype(jnp.int32)


def _route_plan(idx, n_blocks):
    nb_pad = -(-n_blocks // LANES) * LANES
    return pl.pallas_call(
        _route_plan_kernel,
        grid=(1,),
        in_specs=[_full(idx)],
        out_specs=[pl.BlockSpec(idx.shape, lambda i: (0, 0, 0)),
                   pl.BlockSpec((1, nb_pad), lambda i: (0, 0)),
                   pl.BlockSpec((1, LANES), lambda i: (0, 0))],
        out_shape=[jax.ShapeDtypeStruct(idx.shape, jnp.int32),
                   jax.ShapeDtypeStruct((1, nb_pad), jnp.int32),
                   jax.ShapeDtypeStruct((1, LANES), jnp.int32)],
        scratch_shapes=[pltpu.VMEM(idx.shape, jnp.int32),
                        pltpu.VMEM((N_EXPERTS, LANES), jnp.float32)],
        compiler_params=_params(("arbitrary",)),
        name="route_plan",
    )(idx)


def _route_smem_spec():
    return pl.BlockSpec((None, ROUTE_ROWS, ROUTE_TILE), lambda i: (i, 0, 0), memory_space=pltpu.SMEM)


def _dispatch_kernel(dest_ref, h_ref, xs_in_hbm, xs_hbm, sem):
    del xs_in_hbm

    def start(t, carry):
        for k in range(TOP_K):
            pltpu.make_async_copy(h_ref.at[pl.ds(t, 1)], xs_hbm.at[pl.ds(dest_ref[k, t], 1)], sem).start()
        return carry

    lax.fori_loop(0, ROUTE_TILE, start, 0, unroll=ISSUE_UNROLL)
    for k in range(TOP_K):
        pltpu.make_async_copy(h_ref, xs_hbm.at[pl.ds(0, ROUTE_TILE)], sem).wait()


def _dispatch(dest, h, n_rows):
    T, D = h.shape
    xs_zero = jnp.zeros((n_rows, D), h.dtype)
    return pl.pallas_call(
        _dispatch_kernel,
        grid=(T // ROUTE_TILE,),
        in_specs=[_route_smem_spec(),
                  pl.BlockSpec((ROUTE_TILE, D), lambda i: (i, 0)),
                  pl.BlockSpec(memory_space=pl.ANY)],
        out_specs=pl.BlockSpec(memory_space=pl.ANY),
        out_shape=jax.ShapeDtypeStruct((n_rows, D), h.dtype),
        scratch_shapes=[pltpu.SemaphoreType.DMA(())],
        input_output_aliases={2: 0},
        compiler_params=_params(("arbitrary",)),
        name="dispatch",
    )(dest, h, xs_zero)


def _moe_ffn_kernel(be_ref, nused_ref, xs_ref, wgu_ref, wdn_ref, ys_ref):
    del be_ref
    b = pl.program_id(0)
    ff = wdn_ref.shape[0]

    @pl.when(b < nused_ref[0])
    def _():
        xb = xs_ref[...].astype(jnp.bfloat16)
        gu = _dot(xb, wgu_ref[...])
        gate = gu[:, :ff]
        up = gu[:, ff:]
        hidden = gate * _sigmoid(gate) * up
        ys_ref[...] = _dot(hidden.astype(jnp.bfloat16), wdn_ref[...])

    @pl.when(b >= nused_ref[0])
    def _():
        ys_ref[...] = jnp.zeros_like(ys_ref)


def _moe_ffn(block_expert, n_used, xs, w_gu, w_dn):
    P, D = xs.shape
    n_blocks = P // MOE_BLOCK
    ff2 = w_gu.shape[2]
    ff = w_dn.shape[1]

    def row_map(b, be, nu):
        return (jnp.minimum(b, nu[0] - 1), 0)

    def w_map(b, be, nu):
        return (be[jnp.minimum(b, nu[0] - 1)], 0, 0)

    grid_spec = pltpu.PrefetchScalarGridSpec(
        num_scalar_prefetch=2,
        grid=(n_blocks,),
        in_specs=[pl.BlockSpec((MOE_BLOCK, D), row_map),
                  pl.BlockSpec((None, D, ff2), w_map),
                  pl.BlockSpec((None, ff, D), w_map)],
        out_specs=pl.BlockSpec((MOE_BLOCK, D), lambda b, be, nu: (b, 0)),
    )
    return pl.pallas_call(
        _moe_ffn_kernel,
        grid_spec=grid_spec,
        out_shape=jax.ShapeDtypeStruct((P, D), jnp.float32),
        compiler_params=_params(("arbitrary",)),
        name="moe_ffn",
    )(block_expert, n_used, xs, w_gu, w_dn)


def _combine_ln2_kernel(dest_ref, h_ref, wts_ref, g2_ref, b2_ref, ys_hbm, o_ref, buf_ref, sem):
    def start(t, carry):
        for k in range(TOP_K):
            pltpu.make_async_copy(ys_hbm.at[pl.ds(dest_ref[k, t], 1)], buf_ref.at[k, pl.ds(t, 1)], sem).start()
        return carry

    lax.fori_loop(0, ROUTE_TILE, start, 0, unroll=ISSUE_UNROLL)
    for k in range(TOP_K):
        pltpu.make_async_copy(ys_hbm.at[pl.ds(0, ROUTE_TILE)], buf_ref.at[k], sem).wait()

    w = wts_ref[...]
    ffn = w[:, 0:1] * buf_ref[0] + w[:, 1:2] * buf_ref[1]
    o_ref[...] = _layer_norm(ALPHA * h_ref[...] + ffn, g2_ref[...], b2_ref[...])


def _combine_ln2(dest, h, wts_t, g2, b2, ys):
    T, D = h.shape
    tc = ROUTE_TILE
    return pl.pallas_call(
        _combine_ln2_kernel,
        grid=(T // tc,),
        in_specs=[_route_smem_spec(),
                  pl.BlockSpec((tc, D), lambda i: (i, 0)),
                  pl.BlockSpec((tc, ROUTE_ROWS), lambda i: (i, 0)),
                  _full(g2), _full(b2),
                  pl.BlockSpec(memory_space=pl.ANY)],
        out_specs=pl.BlockSpec((tc, D), lambda i: (i, 0)),
        out_shape=jax.ShapeDtypeStruct((T, D), jnp.float32),
        scratch_shapes=[pltpu.VMEM((TOP_K, tc, D), jnp.float32), pltpu.SemaphoreType.DMA(())],
        compiler_params=_params(("arbitrary",)),
        name="combine_ln2",
    )(dest, h, wts_t, g2, b2, ys)


def _layer(x, positions, w_in, gate_bias, lam_vecs, subln_gain, w_pool, pool_scale, w_branch_a, w_branch_b,
           w_out, ln1_gain, ln1_bias, w_router_group, b_router_group, w_router_expert, b_router_expert,
           w_gate_up, w_down, ln2_gain, ln2_bias, lambda_init):
    B, S, D = x.shape
    T = B * S
    attn_w = w_branch_a.shape[0]
    pool_w = w_branch_b.shape[0]
    assert D == attn_w and pool_w == len(POOL_WINDOWS) * LANES and S % ATTN_TILE == 0 and T % ROUTE_TILE == 0
    bf = jnp.bfloat16
    row = lambda a: a.reshape(1, -1).astype(jnp.float32)

    x2 = x.reshape(T, D)
    pos2 = positions.reshape(T, 1)
    c_k, c_v, c_u = 2 * attn_w, 3 * attn_w, 3 * attn_w + pool_w
    n_heads = attn_w // VALUE_DIM
    g_, half_, hh_, mp_, i_ = jnp.meshgrid(jnp.arange(n_heads // HEADS_PER_PAIR), jnp.arange(2),
                                           jnp.arange(HEADS_PER_PAIR), jnp.arange(2), jnp.arange(ROPE_HALF),
                                           indexing="ij")
    perm = ((g_ * HEADS_PER_PAIR + hh_) * VALUE_DIM + mp_ * HEAD_DIM + half_ * ROPE_HALF + i_).reshape(-1)
    w_qk = jnp.concatenate([w_in[:, :attn_w][:, perm], w_in[:, attn_w:c_k][:, perm]], axis=1).astype(bf)
    w_vT = w_in[:, c_k:c_v].T.astype(bf)
    w_u = w_in[:, c_v:c_u].astype(bf)
    w_ga = w_in[:, c_u:c_u + D].astype(bf)
    w_gb = w_in[:, c_u + D:].astype(bf)
    b_ga = row(gate_bias[:D])
    b_gb = row(gate_bias[D:])
    half = HEAD_DIM // 2
    inv_freq = ROPE_THETA ** (-jnp.arange(half, dtype=jnp.float32) * (2.0 / HEAD_DIM))
    inv_freq = jnp.tile(inv_freq, LANES // half).reshape(1, LANES)

    tm = min(512, S)
    qk = _qk_proj(x2, pos2, inv_freq, w_qk, tm, tn=512)
    vT = _v_proj(x2, w_vT, B, S, ATTN_TILE)
    zb = _pool_branch(x2, w_u, w_gb, b_gb, w_pool.astype(bf), row(pool_scale), w_branch_b.astype(bf), S, tm)
    o = _diff_attn(qk, vT, lam_vecs.astype(jnp.float32), subln_gain.reshape(-1, 1).astype(jnp.float32),
                   B, S, lambda_init)

    n_route = N_EXPERTS + N_GROUPS
    n_route_pad = -(-n_route // SUBLANES) * SUBLANES
    w_rT = jnp.concatenate([w_router_expert, w_router_group], axis=1).T.astype(jnp.float32)
    w_rT = jnp.pad(w_rT, ((0, n_route_pad - n_route), (0, 0)))
    b_r = jnp.pad(jnp.concatenate([b_router_expert, b_router_group]).astype(jnp.float32),
                  (0, n_route_pad - n_route)).reshape(-1, 1)
    h, idx, wts = _merge_ln1(x2, o, zb, w_branch_a.astype(bf), w_ga, b_ga, w_out.astype(bf),
                             row(ln1_gain), row(ln1_bias), w_rT, b_r)

    n_blocks = -(-(T * TOP_K) // MOE_BLOCK) + N_EXPERTS
    dest, block_expert, n_used = _route_plan(idx, n_blocks)
    xs = _dispatch(dest, h, n_blocks * MOE_BLOCK)
    ys = _moe_ffn(block_expert[0, :n_blocks], n_used[0, :1], xs, w_gate_up.astype(bf), w_down.astype(bf))
    wts_t = wts.transpose(0, 2, 1).reshape(T, ROUTE_ROWS)
    out = _combine_ln2(dest, h, wts_t, row(ln2_gain), row(ln2_bias), ys)
    return out.reshape(B, S, D)


def kernel(x, positions, w_in, gate_bias, lam_vecs, subln_gain, w_pool, pool_scale, w_branch_a, w_branch_b,
           w_out, ln1_gain, ln1_bias, w_router_group, b_router_group, w_router_expert, b_router_expert,
           w_gate_up, w_down, ln2_gain, ln2_bias):
    for layer in range(w_in.shape[0]):
        lambda_init = 0.8 - 0.6 * math.exp(-0.3 * layer)
        x = _layer(x, positions, w_in[layer], gate_bias[layer], lam_vecs[layer], subln_gain[layer],
                   w_pool[layer], pool_scale[layer], w_branch_a[layer], w_branch_b[layer], w_out[layer],
                   ln1_gain[layer], ln1_bias[layer], w_router_group[layer], b_router_group[layer],
                   w_router_expert[layer], b_router_expert[layer], w_gate_up[layer], w_down[layer],
                   ln2_gain[layer], ln2_bias[layer], lambda_init)
    return x
```

```python
import functools
import math

import jax
import jax.numpy as jnp
from jax import lax
from jax.experimental import pallas as pl
from jax.experimental.pallas import tpu as pltpu

HEAD_DIM = 64
VALUE_DIM = 2 * HEAD_DIM
CHUNK = 64
ROPE_THETA = 10000.0
POOL_WINDOWS = (2, 4, 8, 16)
POOL_HALO = 16
N_GROUPS = 4
EXPERTS_PER_GROUP = 8
N_EXPERTS = N_GROUPS * EXPERTS_PER_GROUP
TOP_K = 2
MOE_BLOCK = 256
LN_EPS = 1e-5
RMS_EPS = 1e-5
DEPTH = 1
ALPHA = (2.0 * DEPTH) ** 0.25
NEG_INF = -1e30

LANES = 128
SUBLANES = 8
VMEM_LIMIT_BYTES = 56 * 1024 * 1024

ATTN_TILE = 256
HEADS_PER_PAIR = 2
PAIR_WIDTH = HEADS_PER_PAIR * VALUE_DIM
ROPE_HALF = HEAD_DIM // 2
LOG2_E = math.log2(math.e)
ROUTE_ROWS = 8
ROUTE_TILE = 256
ISSUE_UNROLL = 8


def _params(semantics):
    return pltpu.CompilerParams(dimension_semantics=semantics, vmem_limit_bytes=VMEM_LIMIT_BYTES)


def _full(a):
    return pl.BlockSpec(a.shape, lambda *_: (0,) * a.ndim)


def _dot(a, b):
    return jnp.dot(a, b, preferred_element_type=jnp.float32)


def _dot_nt(a, b):
    return lax.dot_general(a, b, (((1,), (1,)), ((), ())), preferred_element_type=jnp.float32)


def _sigmoid(x):
    return 1.0 / (1.0 + jnp.exp(-x))


def _layer_norm(r, gain, bias):
    mu = jnp.mean(r, axis=-1, keepdims=True)
    c = r - mu
    var = jnp.mean(c * c, axis=-1, keepdims=True)
    return c * lax.rsqrt(var + LN_EPS) * gain + bias


def _qk_proj_kernel(x_ref, pos_ref, invf_ref, w_ref, o_ref, xb_ref, cos_ref, sin_ref, *, n_q_tiles):
    j = pl.program_id(1)

    @pl.when(j == 0)
    def _():
        xb_ref[...] = x_ref[...].astype(jnp.bfloat16)
        ang = pos_ref[...].astype(jnp.float32) * invf_ref[...]
        cos_ref[...] = jnp.cos(ang)
        sin_ref[...] = jnp.sin(ang)

    scale = jnp.where(j < n_q_tiles, HEAD_DIM ** -0.5 * LOG2_E, 1.0)
    cos = cos_ref[...] * scale
    sin = sin_ref[...] * scale
    for c in range(w_ref.shape[1] // PAIR_WIDTH):
        acc = _dot(xb_ref[...], w_ref[:, c * PAIR_WIDTH:(c + 1) * PAIR_WIDTH])
        t1 = acc[:, :LANES]
        t2 = acc[:, LANES:]
        o_ref[:, c * PAIR_WIDTH:c * PAIR_WIDTH + LANES] = (t1 * cos - t2 * sin).astype(o_ref.dtype)
        o_ref[:, c * PAIR_WIDTH + LANES:(c + 1) * PAIR_WIDTH] = (t2 * cos + t1 * sin).astype(o_ref.dtype)


def _qk_proj(x2, pos2, inv_freq, w_qk, tm, tn):
    T, D = x2.shape
    n_cols = w_qk.shape[1]
    kern = functools.partial(_qk_proj_kernel, n_q_tiles=(n_cols // 2) // tn)
    return pl.pallas_call(
        kern,
        grid=(T // tm, n_cols // tn),
        in_specs=[
            pl.BlockSpec((tm, D), lambda i, j: (i, 0)),
            pl.BlockSpec((tm, 1), lambda i, j: (i, 0)),
            pl.BlockSpec((1, LANES), lambda i, j: (0, 0)),
            pl.BlockSpec((D, tn), lambda i, j: (0, j)),
        ],
        out_specs=pl.BlockSpec((tm, tn), lambda i, j: (i, j)),
        out_shape=jax.ShapeDtypeStruct((T, n_cols), jnp.bfloat16),
        scratch_shapes=[
            pltpu.VMEM((tm, D), jnp.bfloat16),
            pltpu.VMEM((tm, LANES), jnp.float32),
            pltpu.VMEM((tm, LANES), jnp.float32),
        ],
        compiler_params=_params(("parallel", "arbitrary")),
        name="qk_proj",
    )(x2, pos2, inv_freq, w_qk)


def _v_proj_kernel(x_ref, w_ref, o_ref):
    xb = x_ref[...].astype(jnp.bfloat16)
    vt = _dot_nt(w_ref[...], xb)
    o_ref[...] = vt.reshape(o_ref.shape).astype(o_ref.dtype)


def _v_proj(x2, w_vT, batch, seq, kt):
    T, D = x2.shape
    n_heads = w_vT.shape[0] // VALUE_DIM
    tiles_per_seq = seq // kt
    return pl.pallas_call(
        _v_proj_kernel,
        grid=(T // kt,),
        in_specs=[pl.BlockSpec((kt, D), lambda i: (i, 0)), _full(w_vT)],
        out_specs=pl.BlockSpec((None, n_heads, VALUE_DIM, kt),
                               lambda i: (i // tiles_per_seq, 0, 0, i % tiles_per_seq)),
        out_shape=jax.ShapeDtypeStruct((batch, n_heads, VALUE_DIM, seq), jnp.bfloat16),
        compiler_params=_params(("parallel",)),
        name="v_proj",
    )(x2, w_vT)


def _pool_branch_kernel(x_ref, wu_ref, wg_ref, bg_ref, wpool_ref, pscale_ref, wb_ref, o_ref, ext_ref,
                        *, tiles_per_seq):
    i = pl.program_id(0)
    tm = x_ref.shape[0]
    t_in_seq = (i % tiles_per_seq) * tm

    @pl.when(i % tiles_per_seq == 0)
    def _():
        ext_ref[0:POOL_HALO, :] = jnp.zeros((POOL_HALO, ext_ref.shape[1]), jnp.float32)

    xb = x_ref[...].astype(jnp.bfloat16)
    u = _dot(xb, wu_ref[...])
    ext_ref[POOL_HALO:, :] = u
    ext = ext_ref[...]

    row = lax.broadcasted_iota(jnp.int32, (tm, 1), 0) + t_in_seq
    mixed = []
    for g, w in enumerate(POOL_WINDOWS):
        s = ext[:, g * LANES:(g + 1) * LANES]
        k = 1
        while k < w:
            s = s + pltpu.roll(s, k, 0)
            k *= 2
        count = jnp.minimum(row + 1, w).astype(jnp.float32)
        pooled = s[POOL_HALO:, :] / count - u[:, g * LANES:(g + 1) * LANES]
        mixed.append(_dot(pooled.astype(jnp.bfloat16), wpool_ref[g]))
    mixed = jnp.concatenate(mixed, axis=1) * pscale_ref[...]
    ext_ref[0:POOL_HALO, :] = u[tm - POOL_HALO:, :]

    y_b = _dot(mixed.astype(jnp.bfloat16), wb_ref[...])
    gate_b = _sigmoid(_dot(xb, wg_ref[...]) + bg_ref[...])
    o_ref[...] = (gate_b * y_b).astype(o_ref.dtype)


def _pool_branch(x2, w_u, w_gb, b_gb, w_pool, pool_scale, w_bb, seq, tm):
    T, D = x2.shape
    P = w_u.shape[1]
    kern = functools.partial(_pool_branch_kernel, tiles_per_seq=seq // tm)
    return pl.pallas_call(
        kern,
        grid=(T // tm,),
        in_specs=[pl.BlockSpec((tm, D), lambda i: (i, 0)), _full(w_u), _full(w_gb), _full(b_gb),
                  _full(w_pool), _full(pool_scale), _full(w_bb)],
        out_specs=pl.BlockSpec((tm, D), lambda i: (i, 0)),
        out_shape=jax.ShapeDtypeStruct((T, D), jnp.bfloat16),
        scratch_shapes=[pltpu.VMEM((tm + POOL_HALO, P), jnp.float32)],
        compiler_params=_params(("arbitrary",)),
        name="pool_branch",
    )(x2, w_u, w_gb, b_gb, w_pool, pool_scale, w_bb)


def _diff_attn_kernel(q_ref, k_ref, vt_ref, lam_ref, gain_ref, o_ref, s_ref, p_ref, *, lambda_init):
    qb = kt = ATTN_TILE
    n_maps = 2 * HEADS_PER_PAIR
    nq = q_ref.shape[0] // qb

    lv = lam_ref[...]
    lam = (jnp.exp(jnp.sum(lv[0:1] * lv[1:2], axis=1, keepdims=True))
           - jnp.exp(jnp.sum(lv[2:3] * lv[3:4], axis=1, keepdims=True)) + lambda_init)
    gain = gain_ref[...] * (1.0 - lambda_init)

    slot = (lax.broadcasted_iota(jnp.int32, (qb, PAIR_WIDTH), 1) % LANES) // ROPE_HALF
    key_chunk = lax.broadcasted_iota(jnp.int32, (kt, n_maps * qb), 0) // CHUNK
    q_chunk = (lax.broadcasted_iota(jnp.int32, (kt, n_maps * qb), 1) % qb) // CHUNK
    allowed = key_chunk <= q_chunk

    for qi in range(nq):
        buf = qi % 2
        kv = (qi + 1) * qb
        diag = kv - kt
        q = q_ref[qi * qb:(qi + 1) * qb, :]
        zero = jnp.zeros_like(q)
        qm = jnp.concatenate([jnp.where(slot == mp, q, zero) for mp in range(n_maps)], axis=0)
        if diag > 0:
            s_ref[buf, 0:diag, :] = _dot_nt(k_ref[0:diag, :], qm)
        s_ref[buf, diag:kv, :] = jnp.where(allowed, _dot_nt(k_ref[diag:kv, :], qm), NEG_INF)
        m = jnp.max(s_ref[buf, 0:kv, :], axis=0, keepdims=True)
        p = jnp.exp2(s_ref[buf, 0:kv, :] - m)
        l = jnp.sum(p, axis=0, keepdims=True)
        p_ref[buf, 0:kv, :] = p.astype(jnp.bfloat16)

        for hh in range(HEADS_PER_PAIR):
            c0 = 2 * hh * qb
            acc = _dot(vt_ref[hh, :, 0:kv], p_ref[buf, 0:kv, c0:c0 + 2 * qb])
            o = acc[:, :qb] / l[:, c0:c0 + qb] - lam * (acc[:, qb:] / l[:, c0 + qb:c0 + 2 * qb])
            ms = jnp.mean(o * o, axis=0, keepdims=True)
            o = o * lax.rsqrt(ms + RMS_EPS) * gain
            o_ref[qi * qb:(qi + 1) * qb, hh * VALUE_DIM:(hh + 1) * VALUE_DIM] = o.T.astype(o_ref.dtype)


def _diff_attn(qk, vT, lam_vecs, gain_col, batch, seq, lambda_init):
    T = qk.shape[0]
    n_heads = vT.shape[1]
    n_pairs = n_heads // HEADS_PER_PAIR
    kern = functools.partial(_diff_attn_kernel, lambda_init=lambda_init)
    return pl.pallas_call(
        kern,
        grid=(batch, n_pairs),
        in_specs=[
            pl.BlockSpec((seq, PAIR_WIDTH), lambda b, g: (b, g)),
            pl.BlockSpec((seq, PAIR_WIDTH), lambda b, g: (b, n_pairs + g)),
            pl.BlockSpec((None, HEADS_PER_PAIR, VALUE_DIM, seq), lambda b, g: (b, g, 0, 0)),
            _full(lam_vecs),
            _full(gain_col),
        ],
        out_specs=pl.BlockSpec((seq, PAIR_WIDTH), lambda b, g: (b, g)),
        out_shape=jax.ShapeDtypeStruct((T, n_heads * VALUE_DIM), jnp.bfloat16),
        scratch_shapes=[
            pltpu.VMEM((2, seq, 2 * HEADS_PER_PAIR * ATTN_TILE), jnp.float32),
            pltpu.VMEM((2, seq, 2 * HEADS_PER_PAIR * ATTN_TILE), jnp.bfloat16),
        ],
        compiler_params=_params(("parallel", "parallel")),
        name="diff_attn",
    )(qk, qk, vT, lam_vecs, gain_col)


def _split_bf16(a):
    hi = a.astype(jnp.bfloat16)
    lo = (a - hi.astype(jnp.float32)).astype(jnp.bfloat16)
    return hi, lo


def _merge_ln1_kernel(x_ref, o_ref, zb_ref, wa_ref, wg_ref, bg_ref, wout_ref, g1_ref, b1_ref,
                      wr_ref, br_ref, h_ref, idx_ref, wts_ref):
    x = x_ref[...]
    xb = x.astype(jnp.bfloat16)
    gate_a = _sigmoid(_dot(xb, wg_ref[...]) + bg_ref[...])
    y_a = _dot(o_ref[...], wa_ref[...])
    merged = gate_a * y_a + zb_ref[...].astype(jnp.float32)
    mix = _dot(merged.astype(jnp.bfloat16), wout_ref[...])
    h = _layer_norm(ALPHA * x + mix, g1_ref[...], b1_ref[...])
    h_ref[...] = h

    h_hi, h_lo = _split_bf16(h)
    w_hi, w_lo = _split_bf16(wr_ref[...])
    logits = _dot_nt(w_hi, h_hi) + _dot_nt(w_hi, h_lo) + _dot_nt(w_lo, h_hi) + br_ref[...]

    tm = x.shape[0]
    e_logit = logits[:N_EXPERTS]
    g_logit = logits[N_EXPERTS:N_EXPERTS + N_GROUPS]
    g_row = lax.broadcasted_iota(jnp.int32, g_logit.shape, 0)
    g_max = jnp.max(g_logit, axis=0, keepdims=True)
    g_idx = jnp.min(jnp.where(g_logit == g_max, g_row, N_GROUPS), axis=0, keepdims=True)
    g_p = 1.0 / jnp.sum(jnp.exp(g_logit - g_max), axis=0, keepdims=True)

    sel = jnp.zeros((EXPERTS_PER_GROUP, tm), jnp.float32)
    for g in range(N_GROUPS):
        sel = jnp.where(g_idx == g, e_logit[g * EXPERTS_PER_GROUP:(g + 1) * EXPERTS_PER_GROUP], sel)
    e_row = lax.broadcasted_iota(jnp.int32, sel.shape, 0)
    m1 = jnp.max(sel, axis=0, keepdims=True)
    i1 = jnp.min(jnp.where(sel == m1, e_row, EXPERTS_PER_GROUP), axis=0, keepdims=True)
    rest = jnp.where(e_row == i1, -jnp.inf, sel)
    m2 = jnp.max(rest, axis=0, keepdims=True)
    i2 = jnp.min(jnp.where(rest == m2, e_row, EXPERTS_PER_GROUP), axis=0, keepdims=True)
    p2 = jnp.exp(m2 - m1)
    w1 = g_p / (1.0 + p2)
    w2 = g_p * p2 / (1.0 + p2)

    base = g_idx * EXPERTS_PER_GROUP
    zi = jnp.zeros((ROUTE_ROWS - TOP_K, tm), jnp.int32)
    zf = jnp.zeros((ROUTE_ROWS - TOP_K, tm), jnp.float32)
    idx_ref[...] = jnp.concatenate([base + i1, base + i2, zi], axis=0)
    wts_ref[...] = jnp.concatenate([w1, w2, zf], axis=0)


def _merge_ln1(x2, o, zb, w_a, w_ga, b_ga, w_out, g1, b1, w_rT, b_r):
    T, D = x2.shape
    tm = ROUTE_TILE
    row = pl.BlockSpec((tm, D), lambda i: (i, 0))
    route = pl.BlockSpec((None, ROUTE_ROWS, tm), lambda i: (i, 0, 0))
    return pl.pallas_call(
        _merge_ln1_kernel,
        grid=(T // tm,),
        in_specs=[row, row, row, _full(w_a), _full(w_ga), _full(b_ga), _full(w_out), _full(g1), _full(b1),
                  _full(w_rT), _full(b_r)],
        out_specs=[row, route, route],
        out_shape=[jax.ShapeDtypeStruct((T, D), jnp.float32),
                   jax.ShapeDtypeStruct((T // tm, ROUTE_ROWS, tm), jnp.int32),
                   jax.ShapeDtypeStruct((T // tm, ROUTE_ROWS, tm), jnp.float32)],
        compiler_params=_params(("parallel",)),
        name="merge_ln1",
    )(x2, o, zb, w_a, w_ga, b_ga, w_out, g1, b1, w_rT, b_r)


def _route_plan_kernel(idx_ref, dest_ref, be_ref, nused_ref, rank_ref, run_ref):
    n_tiles = idx_ref.shape[0]
    c = ROUTE_TILE
    e_col = lax.broadcasted_iota(jnp.int32, (N_EXPERTS, c), 0)
    upper = (lax.broadcasted_iota(jnp.int32, (c, c), 0) < lax.broadcasted_iota(jnp.int32, (c, c), 1)
             ).astype(jnp.bfloat16)
    run_ref[...] = jnp.zeros_like(run_ref)

    def rank_tile(ci, carry):
        idx = idx_ref[ci]
        oh0 = (e_col == idx[0:1]).astype(jnp.float32)
        oh1 = (e_col == idx[1:2]).astype(jnp.float32)
        both = oh0 + oh1
        run = run_ref[...]
        before = _dot(both.astype(jnp.bfloat16), upper) + run[:, 0:1]
        r0 = jnp.sum(oh0 * before, axis=0, keepdims=True).astype(jnp.int32)
        r1 = jnp.sum(oh1 * before, axis=0, keepdims=True).astype(jnp.int32)
        rank_ref[ci] = jnp.concatenate([r0, r1, jnp.zeros((ROUTE_ROWS - TOP_K, c), jnp.int32)], axis=0)
        run_ref[...] = run + jnp.sum(both, axis=1, keepdims=True)
        return carry

    lax.fori_loop(0, n_tiles, rank_tile, 0)

    counts = run_ref[...]
    padded = jnp.floor((counts + (MOE_BLOCK - 1)) * (1.0 / MOE_BLOCK)) * MOE_BLOCK
    e_row = lax.broadcasted_iota(jnp.int32, padded.shape, 0)
    ends = padded
    k = 1
    while k < N_EXPERTS:
        ends = ends + jnp.where(e_row >= k, pltpu.roll(ends, k, 0), 0.0)
        k *= 2
    start = (ends - padded)[:, 0:1]

    def dest_tile(ci, carry):
        idx = idx_ref[ci]
        rank = rank_ref[ci]
        rows = []
        for k in range(TOP_K):
            oh = (e_col == idx[k:k + 1]).astype(jnp.float32)
            seg = jnp.sum(oh * start, axis=0, keepdims=True).astype(jnp.int32)
            rows.append(seg + rank[k:k + 1])
        rows.append(jnp.zeros((ROUTE_ROWS - TOP_K, c), jnp.int32))
        dest_ref[ci] = jnp.concatenate(rows, axis=0)
        return carry

    lax.fori_loop(0, n_tiles, dest_tile, 0)

    blk = (lax.broadcasted_iota(jnp.int32, (N_EXPERTS, be_ref.shape[1]), 1) * MOE_BLOCK).astype(jnp.float32)
    be = jnp.sum((ends[:, 0:1] <= blk).astype(jnp.int32), axis=0, keepdims=True)
    be_ref[...] = jnp.minimum(be, N_EXPERTS - 1)
    total = ends[N_EXPERTS - 1:N_EXPERTS, :] * (1.0 / MOE_BLOCK)
    nused_ref[...] = total.astype(jnp.int32)


def _route_plan(idx, n_blocks):
    nb_pad = -(-n_blocks // LANES) * LANES
    return pl.pallas_call(
        _route_plan_kernel,
        grid=(1,),
        in_specs=[_full(idx)],
        out_specs=[pl.BlockSpec(idx.shape, lambda i: (0, 0, 0)),
                   pl.BlockSpec((1, nb_pad), lambda i: (0, 0)),
                   pl.BlockSpec((1, LANES), lambda i: (0, 0))],
        out_shape=[jax.ShapeDtypeStruct(idx.shape, jnp.int32),
                   jax.ShapeDtypeStruct((1, nb_pad), jnp.int32),
                   jax.ShapeDtypeStruct((1, LANES), jnp.int32)],
        scratch_shapes=[pltpu.VMEM(idx.shape, jnp.int32),
                        pltpu.VMEM((N_EXPERTS, LANES), jnp.float32)],
        compiler_params=_params(("arbitrary",)),
        name="route_plan",
    )(idx)


def _route_smem_spec():
    return pl.BlockSpec((None, ROUTE_ROWS, ROUTE_TILE), lambda i: (i, 0, 0), memory_space=pltpu.SMEM)


def _dispatch_kernel(dest_ref, h_ref, xs_in_hbm, xs_hbm, sem):
    del xs_in_hbm

    def start(t, carry):
        for k in range(TOP_K):
            pltpu.make_async_copy(h_ref.at[pl.ds(t, 1)], xs_hbm.at[pl.ds(dest_ref[k, t], 1)], sem).start()
        return carry

    lax.fori_loop(0, ROUTE_TILE, start, 0, unroll=ISSUE_UNROLL)
    for k in range(TOP_K):
        pltpu.make_async_copy(h_ref, xs_hbm.at[pl.ds(0, ROUTE_TILE)], sem).wait()


def _dispatch(dest, h, n_rows):
    T, D = h.shape
    xs_zero = jnp.zeros((n_rows, D), h.dtype)
    return pl.pallas_call(
        _dispatch_kernel,
        grid=(T // ROUTE_TILE,),
        in_specs=[_route_smem_spec(),
                  pl.BlockSpec((ROUTE_TILE, D), lambda i: (i, 0)),
                  pl.BlockSpec(memory_space=pl.ANY)],
        out_specs=pl.BlockSpec(memory_space=pl.ANY),
        out_shape=jax.ShapeDtypeStruct((n_rows, D), h.dtype),
        scratch_shapes=[pltpu.SemaphoreType.DMA(())],
        input_output_aliases={2: 0},
        compiler_params=_params(("arbitrary",)),
        name="dispatch",
    )(dest, h, xs_zero)


def _moe_ffn_kernel(be_ref, nused_ref, xs_ref, wgu_ref, wdn_ref, ys_ref, wgu_bf_ref, wdn_bf_ref):
    b = pl.program_id(0)
    ff = wdn_ref.shape[0]
    used = b < nused_ref[0]
    new_expert = jnp.logical_or(b == 0, be_ref[b] != be_ref[jnp.maximum(b - 1, 0)])

    @pl.when(jnp.logical_and(used, new_expert))
    def _():
        wgu_bf_ref[...] = wgu_ref[...].astype(jnp.bfloat16)
        wdn_bf_ref[...] = wdn_ref[...].astype(jnp.bfloat16)

    @pl.when(used)
    def _():
        xb = xs_ref[...].astype(jnp.bfloat16)
        gu = _dot(xb, wgu_bf_ref[...])
        gate = gu[:, :ff]
        up = gu[:, ff:]
        hidden = gate * _sigmoid(gate) * up
        ys_ref[...] = _dot(hidden.astype(jnp.bfloat16), wdn_bf_ref[...])

    @pl.when(b >= nused_ref[0])
    def _():
        ys_ref[...] = jnp.zeros_like(ys_ref)


def _moe_ffn(block_expert, n_used, xs, w_gu, w_dn):
    P, D = xs.shape
    n_blocks = P // MOE_BLOCK
    ff2 = w_gu.shape[2]
    ff = w_dn.shape[1]

    def row_map(b, be, nu):
        return (jnp.minimum(b, nu[0] - 1), 0)

    def w_map(b, be, nu):
        return (be[jnp.minimum(b, nu[0] - 1)], 0, 0)

    grid_spec = pltpu.PrefetchScalarGridSpec(
        num_scalar_prefetch=2,
        grid=(n_blocks,),
        in_specs=[pl.BlockSpec((MOE_BLOCK, D), row_map),
                  pl.BlockSpec((None, D, ff2), w_map),
                  pl.BlockSpec((None, ff, D), w_map)],
        out_specs=pl.BlockSpec((MOE_BLOCK, D), lambda b, be, nu: (b, 0)),
        scratch_shapes=[pltpu.VMEM((D, ff2), jnp.bfloat16), pltpu.VMEM((ff, D), jnp.bfloat16)],
    )
    return pl.pallas_call(
        _moe_ffn_kernel,
        grid_spec=grid_spec,
        out_shape=jax.ShapeDtypeStruct((P, D), jnp.float32),
        compiler_params=_params(("arbitrary",)),
        name="moe_ffn",
    )(block_expert, n_used, xs, w_gu, w_dn)


def _combine_ln2_kernel(dest_ref, dest_next_ref, h_ref, wts_ref, g2_ref, b2_ref, ys_hbm, o_ref, buf_ref, sem):
    i = pl.program_id(0)
    slot = i % 2

    def gather(dref, sl):
        def start(t, carry):
            for k in range(TOP_K):
                pltpu.make_async_copy(ys_hbm.at[pl.ds(dref[k, t], 1)], buf_ref.at[sl, k, pl.ds(t, 1)],
                                      sem.at[sl]).start()
            return carry
        lax.fori_loop(0, ROUTE_TILE, start, 0, unroll=ISSUE_UNROLL)

    @pl.when(i == 0)
    def _():
        gather(dest_ref, 0)

    @pl.when(i + 1 < pl.num_programs(0))
    def _():
        gather(dest_next_ref, 1 - slot)

    for k in range(TOP_K):
        pltpu.make_async_copy(ys_hbm.at[pl.ds(0, ROUTE_TILE)], buf_ref.at[slot, k], sem.at[slot]).wait()

    w = wts_ref[...]
    ffn = w[:, 0:1] * buf_ref[slot, 0] + w[:, 1:2] * buf_ref[slot, 1]
    o_ref[...] = _layer_norm(ALPHA * h_ref[...] + ffn, g2_ref[...], b2_ref[...])


def _combine_ln2(dest, h, wts_t, g2, b2, ys):
    T, D = h.shape
    tc = ROUTE_TILE
    n_tiles = T // tc
    next_spec = pl.BlockSpec((None, ROUTE_ROWS, ROUTE_TILE), lambda i: (jnp.minimum(i + 1, n_tiles - 1), 0, 0),
                             memory_space=pltpu.SMEM)
    return pl.pallas_call(
        _combine_ln2_kernel,
        grid=(n_tiles,),
        in_specs=[_route_smem_spec(), next_spec,
                  pl.BlockSpec((tc, D), lambda i: (i, 0)),
                  pl.BlockSpec((tc, ROUTE_ROWS), lambda i: (i, 0)),
                  _full(g2), _full(b2),
                  pl.BlockSpec(memory_space=pl.ANY)],
        out_specs=pl.BlockSpec((tc, D), lambda i: (i, 0)),
        out_shape=jax.ShapeDtypeStruct((T, D), jnp.float32),
        scratch_shapes=[pltpu.VMEM((2, TOP_K, tc, D), jnp.float32), pltpu.SemaphoreType.DMA((2,))],
        compiler_params=_params(("arbitrary",)),
        name="combine_ln2",
    )(dest, dest, h, wts_t, g2, b2, ys)


def _layer(x, positions, w_in, gate_bias, lam_vecs, subln_gain, w_pool, pool_scale, w_branch_a, w_branch_b,
           w_out, ln1_gain, ln1_bias, w_router_group, b_router_group, w_router_expert, b_router_expert,
           w_gate_up, w_down, ln2_gain, ln2_bias, lambda_init):
    B, S, D = x.shape
    T = B * S
    attn_w = w_branch_a.shape[0]
    pool_w = w_branch_b.shape[0]
    assert D == attn_w and pool_w == len(POOL_WINDOWS) * LANES and S % ATTN_TILE == 0 and T % ROUTE_TILE == 0
    bf = jnp.bfloat16
    row = lambda a: a.reshape(1, -1).astype(jnp.float32)

    x2 = x.reshape(T, D)
    pos2 = positions.reshape(T, 1)
    c_k, c_v, c_u = 2 * attn_w, 3 * attn_w, 3 * attn_w + pool_w
    n_heads = attn_w // VALUE_DIM
    g_, half_, hh_, mp_, i_ = jnp.meshgrid(jnp.arange(n_heads // HEADS_PER_PAIR), jnp.arange(2),
                                           jnp.arange(HEADS_PER_PAIR), jnp.arange(2), jnp.arange(ROPE_HALF),
                                           indexing="ij")
    perm = ((g_ * HEADS_PER_PAIR + hh_) * VALUE_DIM + mp_ * HEAD_DIM + half_ * ROPE_HALF + i_).reshape(-1)
    w_qk = jnp.concatenate([w_in[:, :attn_w][:, perm], w_in[:, attn_w:c_k][:, perm]], axis=1).astype(bf)
    w_vT = w_in[:, c_k:c_v].T.astype(bf)
    w_u = w_in[:, c_v:c_u].astype(bf)
    w_ga = w_in[:, c_u:c_u + D].astype(bf)
    w_gb = w_in[:, c_u + D:].astype(bf)
    b_ga = row(gate_bias[:D])
    b_gb = row(gate_bias[D:])
    half = HEAD_DIM // 2
    inv_freq = ROPE_THETA ** (-jnp.arange(half, dtype=jnp.float32) * (2.0 / HEAD_DIM))
    inv_freq = jnp.tile(inv_freq, LANES // half).reshape(1, LANES)

    tm = min(512, S)
    qk = _qk_proj(x2, pos2, inv_freq, w_qk, tm, tn=512)
    vT = _v_proj(x2, w_vT, B, S, tm)
    zb = _pool_branch(x2, w_u, w_gb, b_gb, w_pool.astype(bf), row(pool_scale), w_branch_b.astype(bf), S, tm)
    o = _diff_attn(qk, vT, lam_vecs.astype(jnp.float32), subln_gain.reshape(-1, 1).astype(jnp.float32),
                   B, S, lambda_init)

    n_route = N_EXPERTS + N_GROUPS
    n_route_pad = -(-n_route // SUBLANES) * SUBLANES
    w_rT = jnp.concatenate([w_router_expert, w_router_group], axis=1).T.astype(jnp.float32)
    w_rT = jnp.pad(w_rT, ((0, n_route_pad - n_route), (0, 0)))
    b_r = jnp.pad(jnp.concatenate([b_router_expert, b_router_group]).astype(jnp.float32),
                  (0, n_route_pad - n_route)).reshape(-1, 1)
    h, idx, wts = _merge_ln1(x2, o, zb, w_branch_a.astype(bf), w_ga, b_ga, w_out.astype(bf),
                             row(ln1_gain), row(ln1_bias), w_rT, b_r)

    n_blocks = -(-(T * TOP_K) // MOE_BLOCK) + N_EXPERTS
    dest, block_expert, n_used = _route_plan(idx, n_blocks)
    xs = _dispatch(dest, h, n_blocks * MOE_BLOCK)
    ys = _moe_ffn(block_expert[0, :n_blocks], n_used[0, :1], xs, w_gate_up.astype(jnp.float32),
                  w_down.astype(jnp.float32))
    wts_t = wts.transpose(0, 2, 1).reshape(T, ROUTE_ROWS)
    out = _combine_ln2(dest, h, wts_t, row(ln2_gain), row(ln2_bias), ys)
    return out.reshape(B, S, D)


def kernel(x, positions, w_in, gate_bias, lam_vecs, subln_gain, w_pool, pool_scale, w_branch_a, w_branch_b,
           w_out, ln1_gain, ln1_bias, w_router_group, b_router_group, w_router_expert, b_router_expert,
           w_gate_up, w_down, ln2_gain, ln2_bias):
    for layer in range(w_in.shape[0]):
        lambda_init = 0.8 - 0.6 * math.exp(-0.3 * layer)
        x = _layer(x, positions, w_in[layer], gate_bias[layer], lam_vecs[layer], subln_gain[layer],
                   w_pool[layer], pool_scale[layer], w_branch_a[layer], w_branch_b[layer], w_out[layer],
                   ln1_gain[layer], ln1_bias[layer], w_router_group[layer], b_router_group[layer],
                   w_router_expert[layer], b_router_expert[layer], w_gate_up[layer], w_down[layer],
                   ln2_gain[layer], ln2_bias[layer], lambda_init)
    return x
```

```python
import functools
import math

import jax
import jax.numpy as jnp
from jax import lax
from jax.experimental import pallas as pl
from jax.experimental.pallas import tpu as pltpu

HEAD_DIM = 64
VALUE_DIM = 2 * HEAD_DIM
CHUNK = 64
ROPE_THETA = 10000.0
POOL_WINDOWS = (2, 4, 8, 16)
POOL_HALO = 16
N_GROUPS = 4
EXPERTS_PER_GROUP = 8
N_EXPERTS = N_GROUPS * EXPERTS_PER_GROUP
TOP_K = 2
MOE_BLOCK = 512
LN_EPS = 1e-5
RMS_EPS = 1e-5
DEPTH = 1
ALPHA = (2.0 * DEPTH) ** 0.25
NEG_INF = -1e30

LANES = 128
SUBLANES = 8
VMEM_LIMIT_BYTES = 56 * 1024 * 1024

ATTN_TILE = 256
HEADS_PER_PAIR = 2
PAIR_WIDTH = HEADS_PER_PAIR * VALUE_DIM
ROPE_HALF = HEAD_DIM // 2
LOG2_E = math.log2(math.e)
ROUTE_ROWS = 8
ROUTE_TILE = 256
MERGE_TILES_PER_STEP = 2


def _params(semantics):
    return pltpu.CompilerParams(dimension_semantics=semantics, vmem_limit_bytes=VMEM_LIMIT_BYTES)


def _full(a):
    return pl.BlockSpec(a.shape, lambda *_: (0,) * a.ndim)


def _dot(a, b):
    return jnp.dot(a, b, preferred_element_type=jnp.float32)


def _dot_nt(a, b):
    return lax.dot_general(a, b, (((1,), (1,)), ((), ())), preferred_element_type=jnp.float32)


def _sigmoid(x):
    return 1.0 / (1.0 + jnp.exp(-x))


def _layer_norm(r, gain, bias):
    mu = jnp.mean(r, axis=-1, keepdims=True)
    c = r - mu
    var = jnp.mean(c * c, axis=-1, keepdims=True)
    return c * lax.rsqrt(var + LN_EPS) * gain + bias


def _qkv_proj_kernel(x_ref, pos_ref, invf_ref, w_ref, wvt_ref, o_ref, vt_ref):
    xb = x_ref[...].astype(jnp.bfloat16)
    ang = pos_ref[...].astype(jnp.float32) * invf_ref[...]
    cos = jnp.cos(ang)
    sin = jnp.sin(ang)
    q_scale = HEAD_DIM ** -0.5 * LOG2_E
    n_groups = w_ref.shape[1] // PAIR_WIDTH
    for c in range(n_groups):
        cs, sn = (cos * q_scale, sin * q_scale) if c < n_groups // 2 else (cos, sin)
        acc = _dot(xb, w_ref[:, c * PAIR_WIDTH:(c + 1) * PAIR_WIDTH])
        t1 = acc[:, :LANES]
        t2 = acc[:, LANES:]
        o_ref[:, c * PAIR_WIDTH:c * PAIR_WIDTH + LANES] = (t1 * cs - t2 * sn).astype(o_ref.dtype)
        o_ref[:, c * PAIR_WIDTH + LANES:(c + 1) * PAIR_WIDTH] = (t2 * cs + t1 * sn).astype(o_ref.dtype)
    vt = _dot_nt(wvt_ref[...], xb)
    vt_ref[...] = vt.reshape(vt_ref.shape).astype(vt_ref.dtype)


def _qkv_proj(x2, pos2, inv_freq, w_qk, w_vT, batch, seq, tm):
    T, D = x2.shape
    n_cols = w_qk.shape[1]
    n_heads = w_vT.shape[0] // VALUE_DIM
    tiles_per_seq = seq // tm
    return pl.pallas_call(
        _qkv_proj_kernel,
        grid=(T // tm,),
        in_specs=[
            pl.BlockSpec((tm, D), lambda i: (i, 0)),
            pl.BlockSpec((tm, 1), lambda i: (i, 0)),
            _full(inv_freq), _full(w_qk), _full(w_vT),
        ],
        out_specs=[pl.BlockSpec((tm, n_cols), lambda i: (i, 0)),
                   pl.BlockSpec((None, n_heads, VALUE_DIM, tm),
                                lambda i: (i // tiles_per_seq, 0, 0, i % tiles_per_seq))],
        out_shape=[jax.ShapeDtypeStruct((T, n_cols), jnp.bfloat16),
                   jax.ShapeDtypeStruct((batch, n_heads, VALUE_DIM, seq), jnp.bfloat16)],
        compiler_params=_params(("parallel",)),
        name="qkv_proj",
    )(x2, pos2, inv_freq, w_qk, w_vT)


def _pool_branch_kernel(x_ref, wu_ref, wg_ref, bg_ref, wpool_ref, pscale_ref, wb_ref, o_ref, ext_ref,
                        *, tiles_per_seq):
    i = pl.program_id(0)
    tm = x_ref.shape[0]
    t_in_seq = (i % tiles_per_seq) * tm

    @pl.when(i % tiles_per_seq == 0)
    def _():
        ext_ref[0:POOL_HALO, :] = jnp.zeros((POOL_HALO, ext_ref.shape[1]), jnp.float32)

    xb = x_ref[...].astype(jnp.bfloat16)
    u = _dot(xb, wu_ref[...])
    ext_ref[POOL_HALO:, :] = u
    ext = ext_ref[...]

    row = lax.broadcasted_iota(jnp.int32, (tm, 1), 0) + t_in_seq
    mixed = []
    for g, w in enumerate(POOL_WINDOWS):
        s = ext[:, g * LANES:(g + 1) * LANES]
        k = 1
        while k < w:
            s = s + pltpu.roll(s, k, 0)
            k *= 2
        count = jnp.minimum(row + 1, w).astype(jnp.float32)
        pooled = s[POOL_HALO:, :] / count - u[:, g * LANES:(g + 1) * LANES]
        mixed.append(_dot(pooled.astype(jnp.bfloat16), wpool_ref[g]))
    mixed = jnp.concatenate(mixed, axis=1) * pscale_ref[...]
    ext_ref[0:POOL_HALO, :] = u[tm - POOL_HALO:, :]

    y_b = _dot(mixed.astype(jnp.bfloat16), wb_ref[...])
    gate_b = _sigmoid(_dot(xb, wg_ref[...]) + bg_ref[...])
    o_ref[...] = (gate_b * y_b).astype(o_ref.dtype)


def _pool_branch(x2, w_u, w_gb, b_gb, w_pool, pool_scale, w_bb, seq, tm):
    T, D = x2.shape
    P = w_u.shape[1]
    kern = functools.partial(_pool_branch_kernel, tiles_per_seq=seq // tm)
    return pl.pallas_call(
        kern,
        grid=(T // tm,),
        in_specs=[pl.BlockSpec((tm, D), lambda i: (i, 0)), _full(w_u), _full(w_gb), _full(b_gb),
                  _full(w_pool), _full(pool_scale), _full(w_bb)],
        out_specs=pl.BlockSpec((tm, D), lambda i: (i, 0)),
        out_shape=jax.ShapeDtypeStruct((T, D), jnp.bfloat16),
        scratch_shapes=[pltpu.VMEM((tm + POOL_HALO, P), jnp.float32)],
        compiler_params=_params(("arbitrary",)),
        name="pool_branch",
    )(x2, w_u, w_gb, b_gb, w_pool, pool_scale, w_bb)


def _diff_attn_kernel(q_ref, k_ref, vt_ref, lam_ref, gain_ref, o_ref, s_ref, p_ref, *, lambda_init):
    qb = kt = ATTN_TILE
    n_maps = 2 * HEADS_PER_PAIR
    nq = q_ref.shape[0] // qb

    lv = lam_ref[...]
    lam = (jnp.exp(jnp.sum(lv[0:1] * lv[1:2], axis=1, keepdims=True))
           - jnp.exp(jnp.sum(lv[2:3] * lv[3:4], axis=1, keepdims=True)) + lambda_init)
    gain = gain_ref[...] * (1.0 - lambda_init)

    slot = (lax.broadcasted_iota(jnp.int32, (qb, PAIR_WIDTH), 1) % LANES) // ROPE_HALF
    key_chunk = lax.broadcasted_iota(jnp.int32, (kt, n_maps * qb), 0) // CHUNK
    q_chunk = (lax.broadcasted_iota(jnp.int32, (kt, n_maps * qb), 1) % qb) // CHUNK
    allowed = key_chunk <= q_chunk

    for qi in range(nq):
        buf = qi % 2
        kv = (qi + 1) * qb
        diag = kv - kt
        q = q_ref[qi * qb:(qi + 1) * qb, :]
        zero = jnp.zeros_like(q)
        qm = jnp.concatenate([jnp.where(slot == mp, q, zero) for mp in range(n_maps)], axis=0)
        if diag > 0:
            s_ref[buf, 0:diag, :] = _dot_nt(k_ref[0:diag, :], qm)
        s_ref[buf, diag:kv, :] = jnp.where(allowed, _dot_nt(k_ref[diag:kv, :], qm), NEG_INF)
        m = jnp.max(s_ref[buf, 0:kv, :], axis=0, keepdims=True)
        p = jnp.exp2(s_ref[buf, 0:kv, :] - m)
        l = jnp.sum(p, axis=0, keepdims=True)
        p_ref[buf, 0:kv, :] = p.astype(jnp.bfloat16)

        for hh in range(HEADS_PER_PAIR):
            c0 = 2 * hh * qb
            acc = _dot(vt_ref[hh, :, 0:kv], p_ref[buf, 0:kv, c0:c0 + 2 * qb])
            o = acc[:, :qb] / l[:, c0:c0 + qb] - lam * (acc[:, qb:] / l[:, c0 + qb:c0 + 2 * qb])
            ms = jnp.mean(o * o, axis=0, keepdims=True)
            o = o * lax.rsqrt(ms + RMS_EPS) * gain
            o_ref[qi * qb:(qi + 1) * qb, hh * VALUE_DIM:(hh + 1) * VALUE_DIM] = o.T.astype(o_ref.dtype)


def _diff_attn(qk, vT, lam_vecs, gain_col, batch, seq, lambda_init):
    T = qk.shape[0]
    n_heads = vT.shape[1]
    n_pairs = n_heads // HEADS_PER_PAIR
    kern = functools.partial(_diff_attn_kernel, lambda_init=lambda_init)
    return pl.pallas_call(
        kern,
        grid=(batch, n_pairs),
        in_specs=[
            pl.BlockSpec((seq, PAIR_WIDTH), lambda b, g: (b, g)),
            pl.BlockSpec((seq, PAIR_WIDTH), lambda b, g: (b, n_pairs + g)),
            pl.BlockSpec((None, HEADS_PER_PAIR, VALUE_DIM, seq), lambda b, g: (b, g, 0, 0)),
            _full(lam_vecs),
            _full(gain_col),
        ],
        out_specs=pl.BlockSpec((seq, PAIR_WIDTH), lambda b, g: (b, g)),
        out_shape=jax.ShapeDtypeStruct((T, n_heads * VALUE_DIM), jnp.bfloat16),
        scratch_shapes=[
            pltpu.VMEM((2, seq, 2 * HEADS_PER_PAIR * ATTN_TILE), jnp.float32),
            pltpu.VMEM((2, seq, 2 * HEADS_PER_PAIR * ATTN_TILE), jnp.bfloat16),
        ],
        compiler_params=_params(("parallel", "parallel")),
        name="diff_attn",
    )(qk, qk, vT, lam_vecs, gain_col)


def _split_bf16(a):
    hi = a.astype(jnp.bfloat16)
    lo = (a - hi.astype(jnp.float32)).astype(jnp.bfloat16)
    return hi, lo


def _merge_ln1_kernel(x_ref, o_ref, zb_ref, wa_ref, wg_ref, bg_ref, wout_ref, g1_ref, b1_ref,
                      wr_ref, br_ref, h_ref, idx_ref, wts_ref):
    for sub in range(idx_ref.shape[0]):
        rows = slice(sub * ROUTE_TILE, (sub + 1) * ROUTE_TILE)
        _merge_ln1_tile(x_ref, o_ref, zb_ref, wa_ref, wg_ref, bg_ref, wout_ref, g1_ref, b1_ref,
                        wr_ref, br_ref, h_ref, idx_ref.at[sub], wts_ref.at[sub], rows)


def _merge_ln1_tile(x_ref, o_ref, zb_ref, wa_ref, wg_ref, bg_ref, wout_ref, g1_ref, b1_ref,
                    wr_ref, br_ref, h_ref, idx_ref, wts_ref, rows):
    x = x_ref[rows, :]
    xb = x.astype(jnp.bfloat16)
    gate_a = _sigmoid(_dot(xb, wg_ref[...]) + bg_ref[...])
    y_a = _dot(o_ref[rows, :], wa_ref[...])
    merged = gate_a * y_a + zb_ref[rows, :].astype(jnp.float32)
    mix = _dot(merged.astype(jnp.bfloat16), wout_ref[...])
    h = _layer_norm(ALPHA * x + mix, g1_ref[...], b1_ref[...])
    h_ref[rows, :] = h

    h_hi, h_lo = _split_bf16(h)
    w_hi, w_lo = _split_bf16(wr_ref[...])
    logits = _dot_nt(w_hi, h_hi) + _dot_nt(w_hi, h_lo) + _dot_nt(w_lo, h_hi) + br_ref[...]

    tm = x.shape[0]
    e_logit = logits[:N_EXPERTS]
    g_logit = logits[N_EXPERTS:N_EXPERTS + N_GROUPS]
    g_row = lax.broadcasted_iota(jnp.int32, g_logit.shape, 0)
    g_max = jnp.max(g_logit, axis=0, keepdims=True)
    g_idx = jnp.min(jnp.where(g_logit == g_max, g_row, N_GROUPS), axis=0, keepdims=True)
    g_p = 1.0 / jnp.sum(jnp.exp(g_logit - g_max), axis=0, keepdims=True)

    sel = jnp.zeros((EXPERTS_PER_GROUP, tm), jnp.float32)
    for g in range(N_GROUPS):
        sel = jnp.where(g_idx == g, e_logit[g * EXPERTS_PER_GROUP:(g + 1) * EXPERTS_PER_GROUP], sel)
    e_row = lax.broadcasted_iota(jnp.int32, sel.shape, 0)
    m1 = jnp.max(sel, axis=0, keepdims=True)
    i1 = jnp.min(jnp.where(sel == m1, e_row, EXPERTS_PER_GROUP), axis=0, keepdims=True)
    rest = jnp.where(e_row == i1, -jnp.inf, sel)
    m2 = jnp.max(rest, axis=0, keepdims=True)
    i2 = jnp.min(jnp.where(rest == m2, e_row, EXPERTS_PER_GROUP), axis=0, keepdims=True)
    p2 = jnp.exp(m2 - m1)
    w1 = g_p / (1.0 + p2)
    w2 = g_p * p2 / (1.0 + p2)

    base = g_idx * EXPERTS_PER_GROUP
    zi = jnp.zeros((ROUTE_ROWS - TOP_K, tm), jnp.int32)
    zf = jnp.zeros((ROUTE_ROWS - TOP_K, tm), jnp.float32)
    idx_ref[...] = jnp.concatenate([base + i1, base + i2, zi], axis=0)
    wts_ref[...] = jnp.concatenate([w1, w2, zf], axis=0)


def _merge_ln1(x2, o, zb, w_a, w_ga, b_ga, w_out, g1, b1, w_rT, b_r):
    T, D = x2.shape
    n_sub = MERGE_TILES_PER_STEP if T % (MERGE_TILES_PER_STEP * ROUTE_TILE) == 0 else 1
    tm = n_sub * ROUTE_TILE
    row = pl.BlockSpec((tm, D), lambda i: (i, 0))
    route = pl.BlockSpec((n_sub, ROUTE_ROWS, ROUTE_TILE), lambda i: (i, 0, 0))
    return pl.pallas_call(
        _merge_ln1_kernel,
        grid=(T // tm,),
        in_specs=[row, row, row, _full(w_a), _full(w_ga), _full(b_ga), _full(w_out), _full(g1), _full(b1),
                  _full(w_rT), _full(b_r)],
        out_specs=[row, route, route],
        out_shape=[jax.ShapeDtypeStruct((T, D), jnp.float32),
                   jax.ShapeDtypeStruct((T // ROUTE_TILE, ROUTE_ROWS, ROUTE_TILE), jnp.int32),
                   jax.ShapeDtypeStruct((T // ROUTE_TILE, ROUTE_ROWS, ROUTE_TILE), jnp.float32)],
        compiler_params=_params(("parallel",)),
        name="merge_ln1",
    )(x2, o, zb, w_a, w_ga, b_ga, w_out, g1, b1, w_rT, b_r)


def _route_plan_kernel(idx_ref, dest_ref, be_ref, nused_ref, rank_ref, run_ref):
    n_tiles = idx_ref.shape[0]
    c = ROUTE_TILE
    e_col = lax.broadcasted_iota(jnp.int32, (N_EXPERTS, c), 0)
    upper = (lax.broadcasted_iota(jnp.int32, (c, c), 0) < lax.broadcasted_iota(jnp.int32, (c, c), 1)
             ).astype(jnp.bfloat16)
    run_ref[...] = jnp.zeros_like(run_ref)

    def rank_tile(ci, carry):
        idx = idx_ref[ci]
        oh0 = (e_col == idx[0:1]).astype(jnp.float32)
        oh1 = (e_col == idx[1:2]).astype(jnp.float32)
        both = oh0 + oh1
        run = run_ref[...]
        before = _dot(both.astype(jnp.bfloat16), upper) + run[:, 0:1]
        r0 = jnp.sum(oh0 * before, axis=0, keepdims=True).astype(jnp.int32)
        r1 = jnp.sum(oh1 * before, axis=0, keepdims=True).astype(jnp.int32)
        rank_ref[ci] = jnp.concatenate([r0, r1, jnp.zeros((ROUTE_ROWS - TOP_K, c), jnp.int32)], axis=0)
        run_ref[...] = run + jnp.sum(both, axis=1, keepdims=True)
        return carry

    lax.fori_loop(0, n_tiles, rank_tile, 0)

    counts = run_ref[...]
    padded = jnp.floor((counts + (MOE_BLOCK - 1)) * (1.0 / MOE_BLOCK)) * MOE_BLOCK
    e_row = lax.broadcasted_iota(jnp.int32, padded.shape, 0)
    ends = padded
    k = 1
    while k < N_EXPERTS:
        ends = ends + jnp.where(e_row >= k, pltpu.roll(ends, k, 0), 0.0)
        k *= 2
    start = (ends - padded)[:, 0:1]

    def dest_tile(ci, carry):
        idx = idx_ref[ci]
        rank = rank_ref[ci]
        rows = []
        for k in range(TOP_K):
            oh = (e_col == idx[k:k + 1]).astype(jnp.float32)
            seg = jnp.sum(oh * start, axis=0, keepdims=True).astype(jnp.int32)
            rows.append(seg + rank[k:k + 1])
        rows.append(jnp.zeros((ROUTE_ROWS - TOP_K, c), jnp.int32))
        dest_ref[ci] = jnp.concatenate(rows, axis=0)
        return carry

    lax.fori_loop(0, n_tiles, dest_tile, 0)

    blk = (lax.broadcasted_iota(jnp.int32, (N_EXPERTS, be_ref.shape[1]), 1) * MOE_BLOCK).astype(jnp.float32)
    be = jnp.sum((ends[:, 0:1] <= blk).astype(jnp.int32), axis=0, keepdims=True)
    be_ref[...] = jnp.minimum(be, N_EXPERTS - 1)
    total = ends[N_EXPERTS - 1:N_EXPERTS, :] * (1.0 / MOE_BLOCK)
    nused_ref[...] = total.astype(jnp.int32)


def _route_plan(idx, n_blocks):
    nb_pad = -(-n_blocks // LANES) * LANES
    return pl.pallas_call(
        _route_plan_kernel,
        grid=(1,),
        in_specs=[_full(idx)],
        out_specs=[pl.BlockSpec(idx.shape, lambda i: (0, 0, 0)),
                   pl.BlockSpec((1, nb_pad), lambda i: (0, 0)),
                   pl.BlockSpec((1, LANES), lambda i: (0, 0))],
        out_shape=[jax.ShapeDtypeStruct(idx.shape, jnp.int32),
                   jax.ShapeDtypeStruct((1, nb_pad), jnp.int32),
                   jax.ShapeDtypeStruct((1, LANES), jnp.int32)],
        scratch_shapes=[pltpu.VMEM(idx.shape, jnp.int32),
                        pltpu.VMEM((N_EXPERTS, LANES), jnp.float32)],
        compiler_params=_params(("arbitrary",)),
        name="route_plan",
    )(idx)


def _route_smem_spec():
    return pl.BlockSpec((None, ROUTE_ROWS, ROUTE_TILE), lambda i: (i, 0, 0), memory_space=pltpu.SMEM)


def _dispatch_kernel(dest_ref, h_ref, xs_in_hbm, xs_hbm, sem):
    del xs_in_hbm

    for t in range(ROUTE_TILE):
        for k in range(TOP_K):
            pltpu.make_async_copy(h_ref.at[pl.ds(t, 1)], xs_hbm.at[pl.ds(dest_ref[k, t], 1)], sem).start()
    for k in range(TOP_K):
        pltpu.make_async_copy(h_ref, xs_hbm.at[pl.ds(0, ROUTE_TILE)], sem).wait()


def _dispatch(dest, h, n_rows):
    T, D = h.shape
    xs_zero = jnp.zeros((n_rows, D), h.dtype)
    return pl.pallas_call(
        _dispatch_kernel,
        grid=(T // ROUTE_TILE,),
        in_specs=[_route_smem_spec(),
                  pl.BlockSpec((ROUTE_TILE, D), lambda i: (i, 0)),
                  pl.BlockSpec(memory_space=pl.ANY)],
        out_specs=pl.BlockSpec(memory_space=pl.ANY),
        out_shape=jax.ShapeDtypeStruct((n_rows, D), h.dtype),
        scratch_shapes=[pltpu.SemaphoreType.DMA(())],
        input_output_aliases={2: 0},
        compiler_params=_params(("arbitrary",)),
        name="dispatch",
    )(dest, h, xs_zero)


def _moe_ffn_kernel(be_ref, nused_ref, xs_ref, wgu_ref, wdn_ref, ys_ref, wgu_bf_ref, wdn_bf_ref):
    b = pl.program_id(0)
    ff = wdn_ref.shape[0]
    used = b < nused_ref[0]
    new_expert = jnp.logical_or(b == 0, be_ref[b] != be_ref[jnp.maximum(b - 1, 0)])

    @pl.when(jnp.logical_and(used, new_expert))
    def _():
        wgu_bf_ref[...] = wgu_ref[...].astype(jnp.bfloat16)
        wdn_bf_ref[...] = wdn_ref[...].astype(jnp.bfloat16)

    @pl.when(used)
    def _():
        xb = xs_ref[...].astype(jnp.bfloat16)
        gu = _dot(xb, wgu_bf_ref[...])
        gate = gu[:, :ff]
        up = gu[:, ff:]
        hidden = gate * _sigmoid(gate) * up
        ys_ref[...] = _dot(hidden.astype(jnp.bfloat16), wdn_bf_ref[...])

    @pl.when(b >= nused_ref[0])
    def _():
        ys_ref[...] = jnp.zeros_like(ys_ref)


def _moe_ffn(block_expert, n_used, xs, w_gu, w_dn):
    P, D = xs.shape
    n_blocks = P // MOE_BLOCK
    ff2 = w_gu.shape[2]
    ff = w_dn.shape[1]

    def row_map(b, be, nu):
        return (jnp.minimum(b, nu[0] - 1), 0)

    def w_map(b, be, nu):
        return (be[jnp.minimum(b, nu[0] - 1)], 0, 0)

    grid_spec = pltpu.PrefetchScalarGridSpec(
        num_scalar_prefetch=2,
        grid=(n_blocks,),
        in_specs=[pl.BlockSpec((MOE_BLOCK, D), row_map),
                  pl.BlockSpec((None, D, ff2), w_map),
                  pl.BlockSpec((None, ff, D), w_map)],
        out_specs=pl.BlockSpec((MOE_BLOCK, D), lambda b, be, nu: (b, 0)),
        scratch_shapes=[pltpu.VMEM((D, ff2), jnp.bfloat16), pltpu.VMEM((ff, D), jnp.bfloat16)],
    )
    return pl.pallas_call(
        _moe_ffn_kernel,
        grid_spec=grid_spec,
        out_shape=jax.ShapeDtypeStruct((P, D), jnp.float32),
        compiler_params=_params(("arbitrary",)),
        name="moe_ffn",
    )(block_expert, n_used, xs, w_gu, w_dn)


def _combine_ln2_kernel(dest_ref, dest_next_ref, h_ref, wts_ref, g2_ref, b2_ref, ys_hbm, o_ref, buf_ref, sem):
    i = pl.program_id(0)
    slot = i % 2

    def gather(dref, sl):
        for t in range(ROUTE_TILE):
            for k in range(TOP_K):
                pltpu.make_async_copy(ys_hbm.at[pl.ds(dref[k, t], 1)], buf_ref.at[sl, k, pl.ds(t, 1)],
                                      sem.at[sl]).start()

    def wait(sl):
        for k in range(TOP_K):
            pltpu.make_async_copy(ys_hbm.at[pl.ds(0, ROUTE_TILE)], buf_ref.at[sl, k], sem.at[sl]).wait()

    @pl.when(i == 0)
    def _():
        gather(dest_ref, 0)

    wait(slot)
    gather(dest_next_ref, 1 - slot)
    w = wts_ref[...]
    ffn = w[:, 0:1] * buf_ref[slot, 0] + w[:, 1:2] * buf_ref[slot, 1]
    o_ref[...] = _layer_norm(ALPHA * h_ref[...] + ffn, g2_ref[...], b2_ref[...])

    @pl.when(i == pl.num_programs(0) - 1)
    def _():
        wait(1 - slot)


def _combine_ln2(dest, h, wts_t, g2, b2, ys):
    T, D = h.shape
    tc = ROUTE_TILE
    n_tiles = T // tc
    next_spec = pl.BlockSpec((None, ROUTE_ROWS, ROUTE_TILE), lambda i: (jnp.minimum(i + 1, n_tiles - 1), 0, 0),
                             memory_space=pltpu.SMEM)
    return pl.pallas_call(
        _combine_ln2_kernel,
        grid=(n_tiles,),
        in_specs=[_route_smem_spec(), next_spec,
                  pl.BlockSpec((tc, D), lambda i: (i, 0)),
                  pl.BlockSpec((tc, ROUTE_ROWS), lambda i: (i, 0)),
                  _full(g2), _full(b2),
                  pl.BlockSpec(memory_space=pl.ANY)],
        out_specs=pl.BlockSpec((tc, D), lambda i: (i, 0)),
        out_shape=jax.ShapeDtypeStruct((T, D), jnp.float32),
        scratch_shapes=[pltpu.VMEM((2, TOP_K, tc, D), jnp.float32), pltpu.SemaphoreType.DMA((2,))],
        compiler_params=_params(("arbitrary",)),
        name="combine_ln2",
    )(dest, dest, h, wts_t, g2, b2, ys)


def _layer(x, positions, w_in, gate_bias, lam_vecs, subln_gain, w_pool, pool_scale, w_branch_a, w_branch_b,
           w_out, ln1_gain, ln1_bias, w_router_group, b_router_group, w_router_expert, b_router_expert,
           w_gate_up, w_down, ln2_gain, ln2_bias, lambda_init):
    B, S, D = x.shape
    T = B * S
    attn_w = w_branch_a.shape[0]
    pool_w = w_branch_b.shape[0]
    assert D == attn_w and pool_w == len(POOL_WINDOWS) * LANES and S % ATTN_TILE == 0 and T % ROUTE_TILE == 0
    bf = jnp.bfloat16
    row = lambda a: a.reshape(1, -1).astype(jnp.float32)

    x2 = x.reshape(T, D)
    pos2 = positions.reshape(T, 1)
    c_k, c_v, c_u = 2 * attn_w, 3 * attn_w, 3 * attn_w + pool_w
    n_heads = attn_w // VALUE_DIM
    g_, half_, hh_, mp_, i_ = jnp.meshgrid(jnp.arange(n_heads // HEADS_PER_PAIR), jnp.arange(2),
                                           jnp.arange(HEADS_PER_PAIR), jnp.arange(2), jnp.arange(ROPE_HALF),
                                           indexing="ij")
    perm = ((g_ * HEADS_PER_PAIR + hh_) * VALUE_DIM + mp_ * HEAD_DIM + half_ * ROPE_HALF + i_).reshape(-1)
    w_qk = jnp.concatenate([w_in[:, :attn_w][:, perm], w_in[:, attn_w:c_k][:, perm]], axis=1).astype(bf)
    w_vT = w_in[:, c_k:c_v].T.astype(bf)
    w_u = w_in[:, c_v:c_u].astype(bf)
    w_ga = w_in[:, c_u:c_u + D].astype(bf)
    w_gb = w_in[:, c_u + D:].astype(bf)
    b_ga = row(gate_bias[:D])
    b_gb = row(gate_bias[D:])
    half = HEAD_DIM // 2
    inv_freq = ROPE_THETA ** (-jnp.arange(half, dtype=jnp.float32) * (2.0 / HEAD_DIM))
    inv_freq = jnp.tile(inv_freq, LANES // half).reshape(1, LANES)

    tm = min(512, S)
    qk, vT = _qkv_proj(x2, pos2, inv_freq, w_qk, w_vT, B, S, tm)
    zb = _pool_branch(x2, w_u, w_gb, b_gb, w_pool.astype(bf), row(pool_scale), w_branch_b.astype(bf), S, tm)
    o = _diff_attn(qk, vT, lam_vecs.astype(jnp.float32), subln_gain.reshape(-1, 1).astype(jnp.float32),
                   B, S, lambda_init)

    n_route = N_EXPERTS + N_GROUPS
    n_route_pad = -(-n_route // SUBLANES) * SUBLANES
    w_rT = jnp.concatenate([w_router_expert, w_router_group], axis=1).T.astype(jnp.float32)
    w_rT = jnp.pad(w_rT, ((0, n_route_pad - n_route), (0, 0)))
    b_r = jnp.pad(jnp.concatenate([b_router_expert, b_router_group]).astype(jnp.float32),
                  (0, n_route_pad - n_route)).reshape(-1, 1)
    h, idx, wts = _merge_ln1(x2, o, zb, w_branch_a.astype(bf), w_ga, b_ga, w_out.astype(bf),
                             row(ln1_gain), row(ln1_bias), w_rT, b_r)

    n_blocks = -(-(T * TOP_K) // MOE_BLOCK) + N_EXPERTS
    dest, block_expert, n_used = _route_plan(idx, n_blocks)
    xs = _dispatch(dest, h, n_blocks * MOE_BLOCK)
    ys = _moe_ffn(block_expert[0, :n_blocks], n_used[0, :1], xs, w_gate_up.astype(jnp.float32),
                  w_down.astype(jnp.float32))
    wts_t = wts.transpose(0, 2, 1).reshape(T, ROUTE_ROWS)
    out = _combine_ln2(dest, h, wts_t, row(ln2_gain), row(ln2_bias), ys)
    return out.reshape(B, S, D)


def kernel(x, positions, w_in, gate_bias, lam_vecs, subln_gain, w_pool, pool_scale, w_branch_a, w_branch_b,
           w_out, ln1_gain, ln1_bias, w_router_group, b_router_group, w_router_expert, b_router_expert,
           w_gate_up, w_down, ln2_gain, ln2_bias):
    for layer in range(w_in.shape[0]):
        lambda_init = 0.8 - 0.6 * math.exp(-0.3 * layer)
        x = _layer(x, positions, w_in[layer], gate_bias[layer], lam_vecs[layer], subln_gain[layer],
                   w_pool[layer], pool_scale[layer], w_branch_a[layer], w_branch_b[layer], w_out[layer],
                   ln1_gain[layer], ln1_bias[layer], w_router_group[layer], b_router_group[layer],
                   w_router_expert[layer], b_router_expert[layer], w_gate_up[layer], w_down[layer],
                   ln2_gain[layer], ln2_bias[layer], lambda_init)
    return x
```

```python
import functools
import math

import jax
import jax.numpy as jnp
from jax import lax
from jax.experimental import pallas as pl
from jax.experimental.pallas import tpu as pltpu

HEAD_DIM = 64
VALUE_DIM = 2 * HEAD_DIM
VALUE_ROWS = VALUE_DIM
CHUNK = 64
ROPE_THETA = 10000.0
POOL_WINDOWS = (2, 4, 8, 16)
POOL_HALO = 16
N_GROUPS = 4
EXPERTS_PER_GROUP = 8
N_EXPERTS = N_GROUPS * EXPERTS_PER_GROUP
TOP_K = 2
MOE_BLOCK = 512
LN_EPS = 1e-5
RMS_EPS = 1e-5
DEPTH = 1
ALPHA = (2.0 * DEPTH) ** 0.25
NEG_INF = -1e30

LANES = 128
SUBLANES = 8
VMEM_LIMIT_BYTES = 56 * 1024 * 1024

ATTN_TILE = 256
HEADS_PER_PAIR = 2
PAIR_WIDTH = HEADS_PER_PAIR * VALUE_DIM
ROPE_HALF = HEAD_DIM // 2
LOG2_E = math.log2(math.e)
ROUTE_ROWS = 8
ROUTE_TILE = 256
MERGE_TILES_PER_STEP = 2


def _params(semantics):
    return pltpu.CompilerParams(dimension_semantics=semantics, vmem_limit_bytes=VMEM_LIMIT_BYTES)


def _full(a):
    return pl.BlockSpec(a.shape, lambda *_: (0,) * a.ndim)


def _dot(a, b):
    return jnp.dot(a, b, preferred_element_type=jnp.float32)


def _dot_nt(a, b):
    return lax.dot_general(a, b, (((1,), (1,)), ((), ())), preferred_element_type=jnp.float32)


def _sigmoid(x):
    return 1.0 / (1.0 + jnp.exp(-x))


def _layer_norm(r, gain, bias):
    mu = jnp.mean(r, axis=-1, keepdims=True)
    c = r - mu
    var = jnp.mean(c * c, axis=-1, keepdims=True)
    return c * lax.rsqrt(var + LN_EPS) * gain + bias


def _qkv_proj_kernel(x_ref, pos_ref, invf_ref, w_ref, wvt_ref, o_ref, vt_ref):
    xb = x_ref[...].astype(jnp.bfloat16)
    ang = pos_ref[...].astype(jnp.float32) * invf_ref[...]
    cos = jnp.cos(ang)
    sin = jnp.sin(ang)
    q_scale = HEAD_DIM ** -0.5 * LOG2_E
    n_groups = w_ref.shape[1] // PAIR_WIDTH
    for c in range(n_groups):
        cs, sn = (cos * q_scale, sin * q_scale) if c < n_groups // 2 else (cos, sin)
        acc = _dot(xb, w_ref[:, c * PAIR_WIDTH:(c + 1) * PAIR_WIDTH])
        t1 = acc[:, :LANES]
        t2 = acc[:, LANES:]
        o_ref[:, c * PAIR_WIDTH:c * PAIR_WIDTH + LANES] = (t1 * cs - t2 * sn).astype(o_ref.dtype)
        o_ref[:, c * PAIR_WIDTH + LANES:(c + 1) * PAIR_WIDTH] = (t2 * cs + t1 * sn).astype(o_ref.dtype)
    vt = _dot_nt(wvt_ref[...], xb)
    vt_ref[...] = vt.reshape(vt_ref.shape).astype(vt_ref.dtype)


def _qkv_proj(x2, pos2, inv_freq, w_qk, w_vT, batch, seq, tm):
    T, D = x2.shape
    n_cols = w_qk.shape[1]
    n_heads = w_vT.shape[0] // VALUE_DIM
    tiles_per_seq = seq // tm
    return pl.pallas_call(
        _qkv_proj_kernel,
        grid=(T // tm,),
        in_specs=[
            pl.BlockSpec((tm, D), lambda i: (i, 0)),
            pl.BlockSpec((tm, 1), lambda i: (i, 0)),
            _full(inv_freq), _full(w_qk), _full(w_vT),
        ],
        out_specs=[pl.BlockSpec((tm, n_cols), lambda i: (i, 0)),
                   pl.BlockSpec((None, n_heads, VALUE_ROWS, tm),
                                lambda i: (i // tiles_per_seq, 0, 0, i % tiles_per_seq))],
        out_shape=[jax.ShapeDtypeStruct((T, n_cols), jnp.bfloat16),
                   jax.ShapeDtypeStruct((batch, n_heads, VALUE_ROWS, seq), jnp.bfloat16)],
        compiler_params=_params(("parallel",)),
        name="qkv_proj",
    )(x2, pos2, inv_freq, w_qk, w_vT)


def _pool_branch_kernel(x_ref, wu_ref, wg_ref, bg_ref, wpool_ref, pscale_ref, wb_ref, o_ref, ext_ref,
                        *, tiles_per_seq):
    i = pl.program_id(0)
    tm = x_ref.shape[0]
    t_in_seq = (i % tiles_per_seq) * tm

    @pl.when(i % tiles_per_seq == 0)
    def _():
        ext_ref[0:POOL_HALO, :] = jnp.zeros((POOL_HALO, ext_ref.shape[1]), jnp.float32)

    xb = x_ref[...].astype(jnp.bfloat16)
    u = _dot(xb, wu_ref[...])
    ext_ref[POOL_HALO:, :] = u
    ext = ext_ref[...]

    row = lax.broadcasted_iota(jnp.int32, (tm, 1), 0) + t_in_seq
    mixed = []
    for g, w in enumerate(POOL_WINDOWS):
        s = ext[:, g * LANES:(g + 1) * LANES]
        k = 1
        while k < w:
            s = s + pltpu.roll(s, k, 0)
            k *= 2
        count = jnp.minimum(row + 1, w).astype(jnp.float32)
        pooled = s[POOL_HALO:, :] / count - u[:, g * LANES:(g + 1) * LANES]
        mixed.append(_dot(pooled.astype(jnp.bfloat16), wpool_ref[g]))
    mixed = jnp.concatenate(mixed, axis=1) * pscale_ref[...]
    ext_ref[0:POOL_HALO, :] = u[tm - POOL_HALO:, :]

    y_b = _dot(mixed.astype(jnp.bfloat16), wb_ref[...])
    gate_b = _sigmoid(_dot(xb, wg_ref[...]) + bg_ref[...])
    o_ref[...] = (gate_b * y_b).astype(o_ref.dtype)


def _pool_branch(x2, w_u, w_gb, b_gb, w_pool, pool_scale, w_bb, seq, tm):
    T, D = x2.shape
    P = w_u.shape[1]
    kern = functools.partial(_pool_branch_kernel, tiles_per_seq=seq // tm)
    return pl.pallas_call(
        kern,
        grid=(T // tm,),
        in_specs=[pl.BlockSpec((tm, D), lambda i: (i, 0)), _full(w_u), _full(w_gb), _full(b_gb),
                  _full(w_pool), _full(pool_scale), _full(w_bb)],
        out_specs=pl.BlockSpec((tm, D), lambda i: (i, 0)),
        out_shape=jax.ShapeDtypeStruct((T, D), jnp.bfloat16),
        scratch_shapes=[pltpu.VMEM((tm + POOL_HALO, P), jnp.float32)],
        compiler_params=_params(("arbitrary",)),
        name="pool_branch",
    )(x2, w_u, w_gb, b_gb, w_pool, pool_scale, w_bb)


def _diff_attn_kernel(q_ref, k_ref, vt_ref, lam_ref, gain_ref, o_ref, s_ref, p_ref, *, lambda_init):
    qb = kt = ATTN_TILE
    n_maps = 2 * HEADS_PER_PAIR
    nq = q_ref.shape[0] // qb

    lv = lam_ref[...]
    lam = (jnp.exp(jnp.sum(lv[0:1] * lv[1:2], axis=1, keepdims=True))
           - jnp.exp(jnp.sum(lv[2:3] * lv[3:4], axis=1, keepdims=True)) + lambda_init)
    gain = gain_ref[...] * (1.0 - lambda_init)

    slot = (lax.broadcasted_iota(jnp.int32, (qb, PAIR_WIDTH), 1) % LANES) // ROPE_HALF
    key_chunk = lax.broadcasted_iota(jnp.int32, (kt, n_maps * qb), 0) // CHUNK
    q_chunk = (lax.broadcasted_iota(jnp.int32, (kt, n_maps * qb), 1) % qb) // CHUNK
    allowed = key_chunk <= q_chunk

    for qi in range(nq):
        buf = qi % 2
        kv = (qi + 1) * qb
        diag = kv - kt
        q = q_ref[qi * qb:(qi + 1) * qb, :]
        zero = jnp.zeros_like(q)
        qm = jnp.concatenate([jnp.where(slot == mp, q, zero) for mp in range(n_maps)], axis=0)
        if diag > 0:
            s_ref[buf, 0:diag, :] = _dot_nt(k_ref[0:diag, :], qm)
        s_ref[buf, diag:kv, :] = jnp.where(allowed, _dot_nt(k_ref[diag:kv, :], qm), NEG_INF)
        m = jnp.max(s_ref[buf, 0:kv, :], axis=0, keepdims=True)
        p = jnp.exp2(s_ref[buf, 0:kv, :] - m)
        l = jnp.sum(p, axis=0, keepdims=True)
        p_ref[buf, 0:kv, :] = p.astype(jnp.bfloat16)

        for hh in range(HEADS_PER_PAIR):
            c0 = 2 * hh * qb
            acc = _dot(vt_ref[hh, :, 0:kv], p_ref[buf, 0:kv, c0:c0 + 2 * qb])
            o = acc[:, :qb] / l[:, c0:c0 + qb] - lam * (acc[:, qb:] / l[:, c0 + qb:c0 + 2 * qb])
            ms = jnp.mean(o * o, axis=0, keepdims=True)
            o = o * lax.rsqrt(ms + RMS_EPS) * gain
            o_ref[qi * qb:(qi + 1) * qb, hh * VALUE_DIM:(hh + 1) * VALUE_DIM] = o.T.astype(o_ref.dtype)


def _diff_attn(qk, vT, lam_vecs, gain_col, batch, seq, lambda_init):
    T = qk.shape[0]
    n_heads = vT.shape[1]
    n_pairs = n_heads // HEADS_PER_PAIR
    kern = functools.partial(_diff_attn_kernel, lambda_init=lambda_init)
    return pl.pallas_call(
        kern,
        grid=(batch, n_pairs),
        in_specs=[
            pl.BlockSpec((seq, PAIR_WIDTH), lambda b, g: (b, g)),
            pl.BlockSpec((seq, PAIR_WIDTH), lambda b, g: (b, n_pairs + g)),
            pl.BlockSpec((None, HEADS_PER_PAIR, VALUE_ROWS, seq), lambda b, g: (b, g, 0, 0)),
            _full(lam_vecs),
            _full(gain_col),
        ],
        out_specs=pl.BlockSpec((seq, PAIR_WIDTH), lambda b, g: (b, g)),
        out_shape=jax.ShapeDtypeStruct((T, n_heads * VALUE_DIM), jnp.bfloat16),
        scratch_shapes=[
            pltpu.VMEM((2, seq, 2 * HEADS_PER_PAIR * ATTN_TILE), jnp.float32),
            pltpu.VMEM((2, seq, 2 * HEADS_PER_PAIR * ATTN_TILE), jnp.bfloat16),
        ],
        compiler_params=_params(("parallel", "parallel")),
        name="diff_attn",
    )(qk, qk, vT, lam_vecs, gain_col)


def _split_bf16(a):
    hi = a.astype(jnp.bfloat16)
    lo = (a - hi.astype(jnp.float32)).astype(jnp.bfloat16)
    return hi, lo


def _merge_ln1_kernel(x_ref, o_ref, zb_ref, wa_ref, wg_ref, bg_ref, wout_ref, g1_ref, b1_ref,
                      wr_ref, br_ref, h_ref, idx_ref, wts_ref):
    for sub in range(idx_ref.shape[0]):
        rows = slice(sub * ROUTE_TILE, (sub + 1) * ROUTE_TILE)
        _merge_ln1_tile(x_ref, o_ref, zb_ref, wa_ref, wg_ref, bg_ref, wout_ref, g1_ref, b1_ref,
                        wr_ref, br_ref, h_ref, idx_ref.at[sub], wts_ref.at[sub], rows)


def _merge_ln1_tile(x_ref, o_ref, zb_ref, wa_ref, wg_ref, bg_ref, wout_ref, g1_ref, b1_ref,
                    wr_ref, br_ref, h_ref, idx_ref, wts_ref, rows):
    x = x_ref[rows, :]
    xb = x.astype(jnp.bfloat16)
    gate_a = _sigmoid(_dot(xb, wg_ref[...]) + bg_ref[...])
    y_a = _dot(o_ref[rows, :], wa_ref[...])
    merged = gate_a * y_a + zb_ref[rows, :].astype(jnp.float32)
    mix = _dot(merged.astype(jnp.bfloat16), wout_ref[...])
    h = _layer_norm(ALPHA * x + mix, g1_ref[...], b1_ref[...])
    h_ref[rows, :] = h

    h_hi, h_lo = _split_bf16(h)
    w_hi, w_lo = _split_bf16(wr_ref[...])
    logits = _dot_nt(w_hi, h_hi) + _dot_nt(w_hi, h_lo) + _dot_nt(w_lo, h_hi) + br_ref[...]

    tm = x.shape[0]
    e_logit = logits[:N_EXPERTS]
    g_logit = logits[N_EXPERTS:N_EXPERTS + N_GROUPS]
    g_row = lax.broadcasted_iota(jnp.int32, g_logit.shape, 0)
    g_max = jnp.max(g_logit, axis=0, keepdims=True)
    g_idx = jnp.min(jnp.where(g_logit == g_max, g_row, N_GROUPS), axis=0, keepdims=True)
    g_p = 1.0 / jnp.sum(jnp.exp(g_logit - g_max), axis=0, keepdims=True)

    sel = jnp.zeros((EXPERTS_PER_GROUP, tm), jnp.float32)
    for g in range(N_GROUPS):
        sel = jnp.where(g_idx == g, e_logit[g * EXPERTS_PER_GROUP:(g + 1) * EXPERTS_PER_GROUP], sel)
    e_row = lax.broadcasted_iota(jnp.int32, sel.shape, 0)
    m1 = jnp.max(sel, axis=0, keepdims=True)
    i1 = jnp.min(jnp.where(sel == m1, e_row, EXPERTS_PER_GROUP), axis=0, keepdims=True)
    rest = jnp.where(e_row == i1, -jnp.inf, sel)
    m2 = jnp.max(rest, axis=0, keepdims=True)
    i2 = jnp.min(jnp.where(rest == m2, e_row, EXPERTS_PER_GROUP), axis=0, keepdims=True)
    p2 = jnp.exp(m2 - m1)
    w1 = g_p / (1.0 + p2)
    w2 = g_p * p2 / (1.0 + p2)

    base = g_idx * EXPERTS_PER_GROUP
    zi = jnp.zeros((ROUTE_ROWS - TOP_K, tm), jnp.int32)
    zf = jnp.zeros((ROUTE_ROWS - TOP_K, tm), jnp.float32)
    idx_ref[...] = jnp.concatenate([base + i1, base + i2, zi], axis=0)
    wts_ref[...] = jnp.concatenate([w1, w2, zf], axis=0)


def _merge_ln1(x2, o, zb, w_a, w_ga, b_ga, w_out, g1, b1, w_rT, b_r):
    T, D = x2.shape
    n_sub = MERGE_TILES_PER_STEP if T % (MERGE_TILES_PER_STEP * ROUTE_TILE) == 0 else 1
    tm = n_sub * ROUTE_TILE
    row = pl.BlockSpec((tm, D), lambda i: (i, 0))
    route = pl.BlockSpec((n_sub, ROUTE_ROWS, ROUTE_TILE), lambda i: (i, 0, 0))
    return pl.pallas_call(
        _merge_ln1_kernel,
        grid=(T // tm,),
        in_specs=[row, row, row, _full(w_a), _full(w_ga), _full(b_ga), _full(w_out), _full(g1), _full(b1),
                  _full(w_rT), _full(b_r)],
        out_specs=[row, route, route],
        out_shape=[jax.ShapeDtypeStruct((T, D), jnp.float32),
                   jax.ShapeDtypeStruct((T // ROUTE_TILE, ROUTE_ROWS, ROUTE_TILE), jnp.int32),
                   jax.ShapeDtypeStruct((T // ROUTE_TILE, ROUTE_ROWS, ROUTE_TILE), jnp.float32)],
        compiler_params=_params(("parallel",)),
        name="merge_ln1",
    )(x2, o, zb, w_a, w_ga, b_ga, w_out, g1, b1, w_rT, b_r)


def _route_plan_kernel(idx_ref, dest_ref, be_ref, nused_ref, rank_ref, run_ref):
    n_tiles = idx_ref.shape[0]
    c = ROUTE_TILE
    e_col = lax.broadcasted_iota(jnp.int32, (N_EXPERTS, c), 0)
    upper = (lax.broadcasted_iota(jnp.int32, (c, c), 0) < lax.broadcasted_iota(jnp.int32, (c, c), 1)
             ).astype(jnp.bfloat16)
    run_ref[...] = jnp.zeros_like(run_ref)

    def rank_tile(ci, carry):
        idx = idx_ref[ci]
        oh0 = (e_col == idx[0:1]).astype(jnp.float32)
        oh1 = (e_col == idx[1:2]).astype(jnp.float32)
        both = oh0 + oh1
        run = run_ref[...]
        before = _dot(both.astype(jnp.bfloat16), upper) + run[:, 0:1]
        r0 = jnp.sum(oh0 * before, axis=0, keepdims=True).astype(jnp.int32)
        r1 = jnp.sum(oh1 * before, axis=0, keepdims=True).astype(jnp.int32)
        rank_ref[ci] = jnp.concatenate([r0, r1, jnp.zeros((ROUTE_ROWS - TOP_K, c), jnp.int32)], axis=0)
        run_ref[...] = run + jnp.sum(both, axis=1, keepdims=True)
        return carry

    lax.fori_loop(0, n_tiles, rank_tile, 0)

    counts = run_ref[...]
    padded = jnp.floor((counts + (MOE_BLOCK - 1)) * (1.0 / MOE_BLOCK)) * MOE_BLOCK
    e_row = lax.broadcasted_iota(jnp.int32, padded.shape, 0)
    ends = padded
    k = 1
    while k < N_EXPERTS:
        ends = ends + jnp.where(e_row >= k, pltpu.roll(ends, k, 0), 0.0)
        k *= 2
    start = (ends - padded)[:, 0:1]

    def dest_tile(ci, carry):
        idx = idx_ref[ci]
        rank = rank_ref[ci]
        rows = []
        for k in range(TOP_K):
            oh = (e_col == idx[k:k + 1]).astype(jnp.float32)
            seg = jnp.sum(oh * start, axis=0, keepdims=True).astype(jnp.int32)
            rows.append(seg + rank[k:k + 1])
        rows.append(jnp.zeros((ROUTE_ROWS - TOP_K, c), jnp.int32))
        dest_ref[ci] = jnp.concatenate(rows, axis=0)
        return carry

    lax.fori_loop(0, n_tiles, dest_tile, 0)

    blk = (lax.broadcasted_iota(jnp.int32, (N_EXPERTS, be_ref.shape[1]), 1) * MOE_BLOCK).astype(jnp.float32)
    be = jnp.sum((ends[:, 0:1] <= blk).astype(jnp.int32), axis=0, keepdims=True)
    be_ref[...] = jnp.minimum(be, N_EXPERTS - 1)
    total = ends[N_EXPERTS - 1:N_EXPERTS, :] * (1.0 / MOE_BLOCK)
    nused_ref[...] = total.astype(jnp.int32)


def _route_plan(idx, n_blocks):
    nb_pad = -(-n_blocks // LANES) * LANES
    return pl.pallas_call(
        _route_plan_kernel,
        grid=(1,),
        in_specs=[_full(idx)],
        out_specs=[pl.BlockSpec(idx.shape, lambda i: (0, 0, 0)),
                   pl.BlockSpec((1, nb_pad), lambda i: (0, 0)),
                   pl.BlockSpec((1, LANES), lambda i: (0, 0))],
        out_shape=[jax.ShapeDtypeStruct(idx.shape, jnp.int32),
                   jax.ShapeDtypeStruct((1, nb_pad), jnp.int32),
                   jax.ShapeDtypeStruct((1, LANES), jnp.int32)],
        scratch_shapes=[pltpu.VMEM(idx.shape, jnp.int32),
                        pltpu.VMEM((N_EXPERTS, LANES), jnp.float32)],
        compiler_params=_params(("arbitrary",)),
        name="route_plan",
    )(idx)


def _route_smem_spec():
    return pl.BlockSpec((None, ROUTE_ROWS, ROUTE_TILE), lambda i: (i, 0, 0), memory_space=pltpu.SMEM)


DISPATCH_SLOTS = 3


def _dispatch_kernel(be_ref, nused_ref, dest_ref, h_hbm, xs_hbm, hbuf_ref, zero_ref, sem_in, sem_out, sem_zero):
    i = pl.program_id(0)
    n = pl.num_programs(0)
    n_blocks = be_ref.shape[0]

    def load(tile, sl):
        return pltpu.make_async_copy(h_hbm.at[pl.ds(tile * ROUTE_TILE, ROUTE_TILE)], hbuf_ref.at[sl], sem_in.at[sl])

    def wait_scatter(sl, parity):
        for k in range(TOP_K):
            pltpu.make_async_copy(hbuf_ref.at[sl], xs_hbm.at[pl.ds(0, ROUTE_TILE)], sem_out.at[parity]).wait()

    @pl.when(i == 0)
    def _():
        zero_ref[...] = jnp.zeros_like(zero_ref)

        def needs_zero(b):
            nxt = be_ref[jnp.minimum(b + 1, n_blocks - 1)]
            return jnp.logical_or(b >= nused_ref[0] - 1, be_ref[b] != nxt)

        def zero_copy(b):
            return pltpu.make_async_copy(zero_ref, xs_hbm.at[pl.ds(b * MOE_BLOCK, MOE_BLOCK)], sem_zero)

        def start_zero(b, carry):
            @pl.when(needs_zero(b))
            def _():
                zero_copy(b).start()
            return carry

        def wait_zero(b, carry):
            @pl.when(needs_zero(b))
            def _():
                zero_copy(b).wait()
            return carry

        lax.fori_loop(0, n_blocks, start_zero, 0)
        lax.fori_loop(0, n_blocks, wait_zero, 0)
        load(0, 0).start()

    slot = i % DISPATCH_SLOTS

    @pl.when(i + 1 < n)
    def _():
        load(i + 1, (i + 1) % DISPATCH_SLOTS).start()

    load(i, slot).wait()
    for t in range(ROUTE_TILE):
        for k in range(TOP_K):
            pltpu.make_async_copy(hbuf_ref.at[slot, pl.ds(t, 1)], xs_hbm.at[pl.ds(dest_ref[k, t], 1)],
                                  sem_out.at[i % 2]).start()

    @pl.when(i > 0)
    def _():
        wait_scatter((i + DISPATCH_SLOTS - 1) % DISPATCH_SLOTS, (i + 1) % 2)

    @pl.when(i == n - 1)
    def _():
        wait_scatter(slot, i % 2)


def _dispatch(block_expert, n_used, dest, h, n_rows):
    T, D = h.shape
    grid_spec = pltpu.PrefetchScalarGridSpec(
        num_scalar_prefetch=2,
        grid=(T // ROUTE_TILE,),
        in_specs=[pl.BlockSpec((None, ROUTE_ROWS, ROUTE_TILE), lambda i, be, nu: (i, 0, 0), memory_space=pltpu.SMEM),
                  pl.BlockSpec(memory_space=pl.ANY)],
        out_specs=pl.BlockSpec(memory_space=pl.ANY),
        scratch_shapes=[pltpu.VMEM((DISPATCH_SLOTS, ROUTE_TILE, D), h.dtype),
                        pltpu.VMEM((MOE_BLOCK, D), h.dtype),
                        pltpu.SemaphoreType.DMA((DISPATCH_SLOTS,)),
                        pltpu.SemaphoreType.DMA((2,)),
                        pltpu.SemaphoreType.DMA(())],
    )
    return pl.pallas_call(
        _dispatch_kernel,
        grid_spec=grid_spec,
        out_shape=jax.ShapeDtypeStruct((n_rows, D), h.dtype),
        compiler_params=_params(("arbitrary",)),
        name="dispatch",
    )(block_expert, n_used, dest, h)


def _moe_ffn_kernel(be_ref, nused_ref, xs_ref, wgu_ref, wdn_ref, ys_ref, wgu_bf_ref, wdn_bf_ref):
    b = pl.program_id(0)
    ff = wdn_ref.shape[0]
    used = b < nused_ref[0]
    new_expert = jnp.logical_or(b == 0, be_ref[b] != be_ref[jnp.maximum(b - 1, 0)])

    @pl.when(jnp.logical_and(used, new_expert))
    def _():
        wgu_bf_ref[...] = wgu_ref[...].astype(jnp.bfloat16)
        wdn_bf_ref[...] = wdn_ref[...].astype(jnp.bfloat16)

    @pl.when(used)
    def _():
        xb = xs_ref[...].astype(jnp.bfloat16)
        gu = _dot(xb, wgu_bf_ref[...])
        gate = gu[:, :ff]
        up = gu[:, ff:]
        hidden = gate * _sigmoid(gate) * up
        ys_ref[...] = _dot(hidden.astype(jnp.bfloat16), wdn_bf_ref[...])

    @pl.when(b >= nused_ref[0])
    def _():
        ys_ref[...] = jnp.zeros_like(ys_ref)


def _moe_ffn(block_expert, n_used, xs, w_gu, w_dn):
    P, D = xs.shape
    n_blocks = P // MOE_BLOCK
    ff2 = w_gu.shape[2]
    ff = w_dn.shape[1]

    def row_map(b, be, nu):
        return (jnp.minimum(b, nu[0] - 1), 0)

    def w_map(b, be, nu):
        return (be[jnp.minimum(b, nu[0] - 1)], 0, 0)

    grid_spec = pltpu.PrefetchScalarGridSpec(
        num_scalar_prefetch=2,
        grid=(n_blocks,),
        in_specs=[pl.BlockSpec((MOE_BLOCK, D), row_map),
                  pl.BlockSpec((None, D, ff2), w_map),
                  pl.BlockSpec((None, ff, D), w_map)],
        out_specs=pl.BlockSpec((MOE_BLOCK, D), lambda b, be, nu: (b, 0)),
        scratch_shapes=[pltpu.VMEM((D, ff2), jnp.bfloat16), pltpu.VMEM((ff, D), jnp.bfloat16)],
    )
    return pl.pallas_call(
        _moe_ffn_kernel,
        grid_spec=grid_spec,
        out_shape=jax.ShapeDtypeStruct((P, D), jnp.float32),
        compiler_params=_params(("arbitrary",)),
        name="moe_ffn",
    )(block_expert, n_used, xs, w_gu, w_dn)


def _combine_ln2_kernel(dest_ref, dest_next_ref, h_ref, wts_ref, g2_ref, b2_ref, ys_hbm, o_ref, buf_ref, sem):
    i = pl.program_id(0)
    slot = i % 2

    def gather(dref, sl):
        for t in range(ROUTE_TILE):
            for k in range(TOP_K):
                pltpu.make_async_copy(ys_hbm.at[pl.ds(dref[k, t], 1)], buf_ref.at[sl, k, pl.ds(t, 1)],
                                      sem.at[sl]).start()

    def wait(sl):
        for k in range(TOP_K):
            pltpu.make_async_copy(ys_hbm.at[pl.ds(0, ROUTE_TILE)], buf_ref.at[sl, k], sem.at[sl]).wait()

    @pl.when(i == 0)
    def _():
        gather(dest_ref, 0)

    wait(slot)
    gather(dest_next_ref, 1 - slot)
    w = wts_ref[...]
    ffn = w[:, 0:1] * buf_ref[slot, 0] + w[:, 1:2] * buf_ref[slot, 1]
    o_ref[...] = _layer_norm(ALPHA * h_ref[...] + ffn, g2_ref[...], b2_ref[...])

    @pl.when(i == pl.num_programs(0) - 1)
    def _():
        wait(1 - slot)


def _combine_ln2(dest, h, wts_t, g2, b2, ys):
    T, D = h.shape
    tc = ROUTE_TILE
    n_tiles = T // tc
    next_spec = pl.BlockSpec((None, ROUTE_ROWS, ROUTE_TILE), lambda i: (jnp.minimum(i + 1, n_tiles - 1), 0, 0),
                             memory_space=pltpu.SMEM)
    return pl.pallas_call(
        _combine_ln2_kernel,
        grid=(n_tiles,),
        in_specs=[_route_smem_spec(), next_spec,
                  pl.BlockSpec((tc, D), lambda i: (i, 0)),
                  pl.BlockSpec((tc, ROUTE_ROWS), lambda i: (i, 0)),
                  _full(g2), _full(b2),
                  pl.BlockSpec(memory_space=pl.ANY)],
        out_specs=pl.BlockSpec((tc, D), lambda i: (i, 0)),
        out_shape=jax.ShapeDtypeStruct((T, D), jnp.float32),
        scratch_shapes=[pltpu.VMEM((2, TOP_K, tc, D), jnp.float32), pltpu.SemaphoreType.DMA((2,))],
        compiler_params=_params(("arbitrary",)),
        name="combine_ln2",
    )(dest, dest, h, wts_t, g2, b2, ys)


def _layer(x, positions, w_in, gate_bias, lam_vecs, subln_gain, w_pool, pool_scale, w_branch_a, w_branch_b,
           w_out, ln1_gain, ln1_bias, w_router_group, b_router_group, w_router_expert, b_router_expert,
           w_gate_up, w_down, ln2_gain, ln2_bias, lambda_init):
    B, S, D = x.shape
    T = B * S
    attn_w = w_branch_a.shape[0]
    pool_w = w_branch_b.shape[0]
    assert D == attn_w and pool_w == len(POOL_WINDOWS) * LANES and S % ATTN_TILE == 0 and T % ROUTE_TILE == 0
    bf = jnp.bfloat16
    row = lambda a: a.reshape(1, -1).astype(jnp.float32)

    x2 = x.reshape(T, D)
    pos2 = positions.reshape(T, 1)
    c_k, c_v, c_u = 2 * attn_w, 3 * attn_w, 3 * attn_w + pool_w
    n_heads = attn_w // VALUE_DIM
    g_, half_, hh_, mp_, i_ = jnp.meshgrid(jnp.arange(n_heads // HEADS_PER_PAIR), jnp.arange(2),
                                           jnp.arange(HEADS_PER_PAIR), jnp.arange(2), jnp.arange(ROPE_HALF),
                                           indexing="ij")
    perm = ((g_ * HEADS_PER_PAIR + hh_) * VALUE_DIM + mp_ * HEAD_DIM + half_ * ROPE_HALF + i_).reshape(-1)
    w_qk = jnp.concatenate([w_in[:, :attn_w][:, perm], w_in[:, attn_w:c_k][:, perm]], axis=1).astype(bf)
    w_vT = w_in[:, c_k:c_v].T.astype(bf)
    w_u = w_in[:, c_v:c_u].astype(bf)
    w_ga = w_in[:, c_u:c_u + D].astype(bf)
    w_gb = w_in[:, c_u + D:].astype(bf)
    b_ga = row(gate_bias[:D])
    b_gb = row(gate_bias[D:])
    half = HEAD_DIM // 2
    inv_freq = ROPE_THETA ** (-jnp.arange(half, dtype=jnp.float32) * (2.0 / HEAD_DIM))
    inv_freq = jnp.tile(inv_freq, LANES // half).reshape(1, LANES)

    tm = min(512, S)
    qk, vT = _qkv_proj(x2, pos2, inv_freq, w_qk, w_vT, B, S, tm)
    zb = _pool_branch(x2, w_u, w_gb, b_gb, w_pool.astype(bf), row(pool_scale), w_branch_b.astype(bf), S, tm)
    o = _diff_attn(qk, vT, lam_vecs.astype(jnp.float32), subln_gain.reshape(-1, 1).astype(jnp.float32),
                   B, S, lambda_init)

    n_route = N_EXPERTS + N_GROUPS
    n_route_pad = -(-n_route // SUBLANES) * SUBLANES
    w_rT = jnp.concatenate([w_router_expert, w_router_group], axis=1).T.astype(jnp.float32)
    w_rT = jnp.pad(w_rT, ((0, n_route_pad - n_route), (0, 0)))
    b_r = jnp.pad(jnp.concatenate([b_router_expert, b_router_group]).astype(jnp.float32),
                  (0, n_route_pad - n_route)).reshape(-1, 1)
    h, idx, wts = _merge_ln1(x2, o, zb, w_branch_a.astype(bf), w_ga, b_ga, w_out.astype(bf),
                             row(ln1_gain), row(ln1_bias), w_rT, b_r)

    n_blocks = -(-(T * TOP_K) // MOE_BLOCK) + N_EXPERTS
    dest, block_expert, n_used = _route_plan(idx, n_blocks)
    block_expert = block_expert[0, :n_blocks]
    n_used = n_used[0, :1]
    xs = _dispatch(block_expert, n_used, dest, h, n_blocks * MOE_BLOCK)
    ys = _moe_ffn(block_expert, n_used, xs, w_gate_up.astype(jnp.float32), w_down.astype(jnp.float32))
    wts_t = wts.transpose(0, 2, 1).reshape(T, ROUTE_ROWS)
    out = _combine_ln2(dest, h, wts_t, row(ln2_gain), row(ln2_bias), ys)
    return out.reshape(B, S, D)


def kernel(x, positions, w_in, gate_bias, lam_vecs, subln_gain, w_pool, pool_scale, w_branch_a, w_branch_b,
           w_out, ln1_gain, ln1_bias, w_router_group, b_router_group, w_router_expert, b_router_expert,
           w_gate_up, w_down, ln2_gain, ln2_bias):
    for layer in range(w_in.shape[0]):
        lambda_init = 0.8 - 0.6 * math.exp(-0.3 * layer)
        x = _layer(x, positions, w_in[layer], gate_bias[layer], lam_vecs[layer], subln_gain[layer],
                   w_pool[layer], pool_scale[layer], w_branch_a[layer], w_branch_b[layer], w_out[layer],
                   ln1_gain[layer], ln1_bias[layer], w_router_group[layer], b_router_group[layer],
                   w_router_expert[layer], b_router_expert[layer], w_gate_up[layer], w_down[layer],
                   ln2_gain[layer], ln2_bias[layer], lambda_init)
    return x
```
